```python
import jax, jax.numpy as jnp
from jax import lax
import numpy as np

D_MODEL = 1024
BATCH = 8
SEQ = 4096
DEPTH = 4

HEAD_DIM = 64
MIX_WIDTH = D_MODEL
MLA_HEADS = MIX_WIDTH // 2 // HEAD_DIM
MLA_WIDTH = MLA_HEADS * HEAD_DIM
NOPE_DIM = HEAD_DIM
ROPE_DIM = HEAD_DIM // 2
V_DIM = HEAD_DIM
Q_RANK = 3 * D_MODEL // 8
KV_RANK = 4 * HEAD_DIM
ROPE_THETA = 10000.0
Q_BLOCK = 128
SG_GROUPS = MIX_WIDTH // 4 // HEAD_DIM
SG_WIDTH = SG_GROUPS * HEAD_DIM
CHUNK = 128
CV_GROUPS = MIX_WIDTH // 4 // HEAD_DIM
CV_WIDTH = CV_GROUPS * HEAD_DIM
CONV_WIDTH = 3
D_FF = ((8 * D_MODEL + 3 * 256 - 1) // (3 * 256)) * 256
EPS = 1e-6
OFF_CQ = 0
OFF_CKV = OFF_CQ + Q_RANK
OFF_KR = OFF_CKV + KV_RANK
OFF_SG = OFF_KR + ROPE_DIM
OFF_CV = OFF_SG + 2 * SG_WIDTH
IN_WIDTH = OFF_CV + 3 * CV_WIDTH

kernel_name = "hybrid_mla_sgu_shortconv_sandwich"


def rms_norm(x, g):
    xf = x.astype(jnp.float32)
    y = xf * lax.rsqrt(jnp.mean(xf * xf, axis=-1, keepdims=True) + EPS)
    return y.astype(x.dtype) * g


def group_layer_norm(x, g, b, groups):
    shp = x.shape
    xf = x.astype(jnp.float32).reshape(shp[:-1] + (groups, shp[-1] // groups))
    mu = jnp.mean(xf, axis=-1, keepdims=True)
    var = jnp.mean(jnp.square(xf - mu), axis=-1, keepdims=True)
    y = ((xf - mu) * lax.rsqrt(var + EPS)).reshape(shp)
    return y.astype(x.dtype) * g + b


def rope_tables(positions):
    inv_freq = 1.0 / (ROPE_THETA ** (jnp.arange(0, ROPE_DIM // 2, dtype=jnp.float32) / (ROPE_DIM // 2)))
    ang = positions.astype(jnp.float32)[..., None] * inv_freq
    return jnp.cos(ang), jnp.sin(ang)


def apply_rope(t, cos, sin):
    tf = t.astype(jnp.float32)
    t1, t2 = jnp.split(tf, 2, axis=-1)
    return jnp.concatenate([t1 * cos - t2 * sin, t2 * cos + t1 * sin], axis=-1).astype(t.dtype)


def causal_latent_attention(q_nope, q_rope, k_nope, k_rope, v):
    b, s, h, _ = q_nope.shape
    nb = s // Q_BLOCK
    scale = (NOPE_DIM + ROPE_DIM) ** -0.5
    kpos = jnp.arange(s)

    def to_blocks(t):
        return jnp.moveaxis(t.reshape((b, nb, Q_BLOCK) + t.shape[2:]), 1, 0)

    def one_block(args):
        qn, qr, i = args
        sc = (jnp.einsum('bqhd,bkhd->bhqk', qn, k_nope)
              + jnp.einsum('bqhd,bkd->bhqk', qr, k_rope)).astype(jnp.float32) * scale
        qpos = i * Q_BLOCK + jnp.arange(Q_BLOCK)
        mask = kpos[None, :] <= qpos[:, None]
        sc = jnp.where(mask, sc, jnp.finfo(jnp.float32).min)
        p = jax.nn.softmax(sc, axis=-1).astype(v.dtype)
        return jnp.einsum('bhqk,bkhd->bqhd', p, v)

    out = lax.map(one_block, (to_blocks(q_nope), to_blocks(q_rope), jnp.arange(nb)))
    return jnp.moveaxis(out, 0, 1).reshape(b, s, h * V_DIM)


def mla_branch(z, cos, sin, q_norm_g, w_uq, kv_norm_g, w_ukv):
    b, s, _ = z.shape
    c_q = rms_norm(z[..., OFF_CQ:OFF_CKV], q_norm_g)
    q = (c_q @ w_uq).reshape(b, s, MLA_HEADS, NOPE_DIM + ROPE_DIM)
    q_nope = q[..., :NOPE_DIM]
    q_rope = apply_rope(q[..., NOPE_DIM:], cos[:, :, None, :], sin[:, :, None, :])
    c_kv = rms_norm(z[..., OFF_CKV:OFF_KR], kv_norm_g)
    kv = (c_kv @ w_ukv).reshape(b, s, MLA_HEADS, NOPE_DIM + V_DIM)
    k_nope, v = kv[..., :NOPE_DIM], kv[..., NOPE_DIM:]
    k_rope = apply_rope(z[..., OFF_KR:OFF_SG], cos, sin)
    return causal_latent_attention(q_nope, q_rope, k_nope, k_rope, v)


def sgu_branch(z, sg_ln_g, sg_ln_b, w_sp, b_sp):
    b, s, _ = z.shape
    uv = jax.nn.gelu(z[..., OFF_SG:OFF_CV])
    u, v = uv[..., :SG_WIDTH], uv[..., SG_WIDTH:]
    v = group_layer_norm(v, sg_ln_g, sg_ln_b, SG_GROUPS)
    vc = v.reshape(b, s // CHUNK, CHUNK, SG_GROUPS, HEAD_DIM)
    w_causal = w_sp * jnp.tril(jnp.ones((CHUNK, CHUNK), w_sp.dtype))
    mixed = jnp.einsum('gts,bcsge->bctge', w_causal, vc) + jnp.swapaxes(b_sp, 0, 1)[:, :, None]
    return u * mixed.reshape(b, s, SG_WIDTH)


def conv_branch(z, conv_w):
    gate_b = z[..., OFF_CV:OFF_CV + CV_WIDTH]
    gate_c = z[..., OFF_CV + CV_WIDTH:OFF_CV + 2 * CV_WIDTH]
    h = z[..., OFF_CV + 2 * CV_WIDTH:IN_WIDTH]
    y = gate_c * h
    yp = jnp.pad(y, ((0, 0), (CONV_WIDTH - 1, 0), (0, 0)))
    s = y.shape[1]
    conv = yp[:, 0:s] * conv_w[0] + yp[:, 1:s + 1] * conv_w[1] + yp[:, 2:s + 2] * conv_w[2]
    return gate_b * conv


def setup_inputs(seed: int = 0) -> dict:
    key = jax.random.key(seed)
    ks = jax.random.split(key, 24)
    L, D = DEPTH, D_MODEL

    def nrm(k, shape, fan_in):
        return jax.random.normal(k, shape, jnp.float32) * fan_in ** -0.5

    def gain(k, shape):
        return 1.0 + 0.05 * jax.random.normal(k, shape, jnp.float32)

    x = jax.random.normal(ks[0], (BATCH, SEQ, D), jnp.float32)
    offsets = jax.random.randint(ks[1], (BATCH, 1), 0, 1024, dtype=jnp.int32)
    positions = (offsets + jnp.arange(SEQ, dtype=jnp.int32)[None, :]).astype(jnp.int32)
    return {
        "x": x,
        "positions": positions,
        "mix_pre_g": gain(ks[2], (L, D)),
        "mix_post_g": gain(ks[3], (L, D)),
        "ffn_pre_g": gain(ks[4], (L, D)),
        "ffn_post_g": gain(ks[5], (L, D)),
        "w_in": nrm(ks[6], (L, D, IN_WIDTH), D),
        "q_norm_g": gain(ks[7], (L, Q_RANK)),
        "w_uq": nrm(ks[8], (L, Q_RANK, MLA_HEADS * (NOPE_DIM + ROPE_DIM)), Q_RANK),
        "kv_norm_g": gain(ks[9], (L, KV_RANK)),
        "w_ukv": nrm(ks[10], (L, KV_RANK, MLA_HEADS * (NOPE_DIM + V_DIM)), KV_RANK),
        "sg_ln_g": gain(ks[11], (L, SG_WIDTH)),
        "sg_ln_b": 0.02 * jax.random.normal(ks[12], (L, SG_WIDTH), jnp.float32),
        "w_sp": nrm(ks[13], (L, SG_GROUPS, CHUNK, CHUNK), CHUNK),
        "b_sp": gain(ks[14], (L, SG_GROUPS, CHUNK)),
        "conv_w": nrm(ks[15], (L, CONV_WIDTH, CV_WIDTH), CONV_WIDTH),
        "out_norm_g": gain(ks[16], (L, MIX_WIDTH)),
        "w_out": nrm(ks[17], (L, MIX_WIDTH, D), MIX_WIDTH),
        "w_gate": nrm(ks[18], (L, D, D_FF), D),
        "w_up": nrm(ks[19], (L, D, D_FF), D),
        "w_down": nrm(ks[20], (L, D_FF, D), D_FF),
    }


def reference(x, positions, mix_pre_g, mix_post_g, ffn_pre_g, ffn_post_g, w_in, q_norm_g, w_uq,
              kv_norm_g, w_ukv, sg_ln_g, sg_ln_b, w_sp, b_sp, conv_w, out_norm_g, w_out,
              w_gate, w_up, w_down):
    cos, sin = rope_tables(positions)
    a_end = MLA_WIDTH
    s_end = MLA_WIDTH + SG_WIDTH
    for l in range(DEPTH):
        h = rms_norm(x, mix_pre_g[l])
        z = h @ w_in[l]
        y_a = mla_branch(z, cos, sin, q_norm_g[l], w_uq[l], kv_norm_g[l], w_ukv[l])
        y_b = sgu_branch(z, sg_ln_g[l], sg_ln_b[l], w_sp[l], b_sp[l])
        y_c = conv_branch(z, conv_w[l])
        g = out_norm_g[l]
        mix = jnp.concatenate([rms_norm(y_a, g[:a_end]),
                               rms_norm(y_b, g[a_end:s_end]),
                               rms_norm(y_c, g[s_end:])], axis=-1)
        x = x + rms_norm(mix @ w_out[l], mix_post_g[l])
        h = rms_norm(x, ffn_pre_g[l])
        f = (jax.nn.silu(h @ w_gate[l]) * (h @ w_up[l])) @ w_down[l]
        x = x + rms_norm(f, ffn_post_g[l])
    return x
```

```python
import functools
import math

import jax
import jax.numpy as jnp
from jax import lax
from jax.experimental import pallas as pl
from jax.experimental.pallas import tpu as pltpu

D_MODEL = 1024
HEAD_DIM = 64
N_HEADS = 8
NOPE_DIM = 64
ROPE_DIM = 32
HALF_ROPE = ROPE_DIM // 2
V_DIM = 64
Q_RANK = 384
KV_RANK = 256
ROPE_THETA = 10000.0
SG_GROUPS = 4
SG_WIDTH = 256
CHUNK = 128
CV_WIDTH = 256
MLA_WIDTH = N_HEADS * V_DIM
D_FF = 2816
EPS = 1e-6

OFF_CQ = 0
OFF_CKV = OFF_CQ + Q_RANK
OFF_KR = OFF_CKV + KV_RANK
OFF_SG = OFF_KR + ROPE_DIM
OFF_CV = OFF_SG + 2 * SG_WIDTH
IN_WIDTH = OFF_CV + 3 * CV_WIDTH

P_CQ = 0
P_CKV = 384
P_KR = 640
P_SG = 768
P_CV = 1280
P_WIDTH = 2048

HEAD_PAD = 128
QK_WIDTH = N_HEADS * HEAD_PAD

VMEM_LIMIT_BYTES = 56 * 1024 * 1024
LANES = 128

TM_IN = 512
T_ATT = 256
TM_OUT = 512
FF_CHUNKS = ((0, 1024), (1024, 1024), (2048, 768))

_NT = (((1,), (1,)), ((), ()))
SOFTMAX_SCALE = (NOPE_DIM + ROPE_DIM) ** -0.5
LOG2E = 1.4426950408889634
MASK_VALUE = -1e30


def _dot(a, b):
    return jnp.dot(a, b, preferred_element_type=jnp.float32)


def _dot_nt(a, b):
    return lax.dot_general(a, b, _NT, preferred_element_type=jnp.float32)


def _rms(x, g):
    return x * lax.rsqrt(jnp.mean(x * x, axis=-1, keepdims=True) + EPS) * g


def _const_spec(shape):
    nd = len(shape)
    return pl.BlockSpec(shape, lambda *_: (0,) * nd, pipeline_mode=pl.Buffered(1))


def _rope_kernel(pos_ref, inv_ref, cos_ref, sin_ref):
    ang = pos_ref[...].astype(jnp.float32) * inv_ref[...]
    cos_ref[...] = jnp.cos(ang)
    sin_ref[...] = jnp.sin(ang)


def _rope_tables(positions):
    t = positions.size
    tn = 4096
    pos = positions.reshape(1, t)
    inv_freq = 1.0 / (ROPE_THETA ** (jnp.arange(0, HALF_ROPE, dtype=jnp.float32) / HALF_ROPE))
    return pl.pallas_call(
        _rope_kernel,
        grid=(t // tn,),
        in_specs=[pl.BlockSpec((1, tn), lambda i: (0, i)),
                  pl.BlockSpec((HALF_ROPE, 1), lambda i: (0, 0))],
        out_specs=[pl.BlockSpec((HALF_ROPE, tn), lambda i: (0, i))] * 2,
        out_shape=[jax.ShapeDtypeStruct((HALF_ROPE, t), jnp.float32)] * 2,
        name="rope_tables",
    )(pos, inv_freq.reshape(HALF_ROPE, 1))


def _inproj_kernel(steps_per_seq,
                   x_ref, g_ref, win_ref, qg_ref, wuqt_ref, kvg_ref, wk_ref, wkr_ref, wvt_ref,
                   tab_ref, cost_ref, sint_ref, lng_ref, lnb_ref, avg_ref, wsp_ref, bsp_ref,
                   convw_ref, ong_ref,
                   qt_ref, k_ref, vt_ref, ybc_ref, carry_ref):
    tm = x_ref.shape[0]
    h = _rms(x_ref[...], g_ref[...]).astype(jnp.bfloat16)
    z = _dot(h, win_ref[...])

    c_q = _rms(z[:, P_CQ:P_CQ + Q_RANK], qg_ref[...]).astype(jnp.bfloat16)
    qt = _dot_nt(wuqt_ref[...], c_q)
    cos_t = cost_ref[...]
    sin_t = sint_ref[...]
    for hd in range(N_HEADS):
        base = hd * HEAD_PAD
        t1 = qt[base + NOPE_DIM:base + NOPE_DIM + HALF_ROPE]
        t2 = qt[base + NOPE_DIM + HALF_ROPE:base + NOPE_DIM + ROPE_DIM]
        qt_ref[base:base + NOPE_DIM, :] = qt[base:base + NOPE_DIM].astype(jnp.bfloat16)
        qt_ref[base + NOPE_DIM:base + NOPE_DIM + HALF_ROPE, :] = (
            t1 * cos_t - t2 * sin_t).astype(jnp.bfloat16)
        qt_ref[base + NOPE_DIM + HALF_ROPE:base + NOPE_DIM + ROPE_DIM, :] = (
            t2 * cos_t + t1 * sin_t).astype(jnp.bfloat16)
        qt_ref[base + NOPE_DIM + ROPE_DIM:base + HEAD_PAD, :] = jnp.zeros(
            (HEAD_PAD - NOPE_DIM - ROPE_DIM, tm), jnp.bfloat16)

    c_kv = _rms(z[:, P_CKV:P_CKV + KV_RANK], kvg_ref[...]).astype(jnp.bfloat16)
    kr_ab = (z[:, P_KR:P_KR + 2 * ROPE_DIM] * tab_ref[...]).astype(jnp.bfloat16)
    k_ref[...] = (_dot(c_kv, wk_ref[...]) + _dot(kr_ab, wkr_ref[...])).astype(jnp.bfloat16)
    vt = _dot_nt(wvt_ref[...], c_kv).astype(jnp.bfloat16)
    tk = vt_ref.shape[2]
    for j in range(tm // tk):
        vt_ref[j] = vt[:, j * tk:(j + 1) * tk]

    uv = z[:, P_SG:P_SG + 2 * SG_WIDTH]
    uv = uv * (0.5 * (1.0 + jnp.tanh(math.sqrt(2.0 / math.pi) * (uv + 0.044715 * (uv * uv * uv)))))
    u = uv[:, :SG_WIDTH]
    v = uv[:, SG_WIDTH:]
    avg = avg_ref[...]

    def group_mean(a):
        hi = a.astype(jnp.bfloat16)
        lo = (a - hi.astype(jnp.float32)).astype(jnp.bfloat16)
        return _dot(hi, avg) + _dot(lo, avg)

    dv = v - group_mean(v)
    vn = dv * lax.rsqrt(group_mean(dv * dv) + EPS) * lng_ref[...] + lnb_ref[...]
    vn = vn.astype(jnp.bfloat16)
    rows = lax.broadcasted_iota(jnp.int32, (SG_GROUPS * CHUNK, CHUNK), 0)
    cols = lax.broadcasted_iota(jnp.int32, (SG_GROUPS * CHUNK, CHUNK), 1)
    w_causal = jnp.where((rows % CHUNK) >= cols, wsp_ref[...], 0.0).astype(jnp.bfloat16)
    lane_group = lax.broadcasted_iota(jnp.int32, (CHUNK, SG_WIDTH), 1) // HEAD_DIM
    bsp = bsp_ref[...]
    mixed = []
    for c in range(tm // CHUNK):
        o = _dot(w_causal, vn[c * CHUNK:(c + 1) * CHUNK])
        m = o[3 * CHUNK:]
        for g in (2, 1, 0):
            m = jnp.where(lane_group == g, o[g * CHUNK:(g + 1) * CHUNK], m)
        mixed.append(m + bsp)
    y_b = u * jnp.concatenate(mixed, axis=0)
    ong = ong_ref[...]
    ybc_ref[:, :SG_WIDTH] = _rms(y_b, ong[:, :SG_WIDTH]).astype(jnp.bfloat16)

    gate_b = z[:, P_CV:P_CV + CV_WIDTH]
    y = z[:, P_CV + CV_WIDTH:P_CV + 2 * CV_WIDTH] * z[:, P_CV + 2 * CV_WIDTH:P_CV + 3 * CV_WIDTH]

    @pl.when(pl.program_id(0) % steps_per_seq == 0)
    def _():
        carry_ref[...] = jnp.zeros_like(carry_ref)

    prev = carry_ref[...]
    carry_ref[...] = y[tm - 8:]
    row8 = lax.broadcasted_iota(jnp.int32, (8, CV_WIDTH), 0)

    def shifted(k):
        r = pltpu.roll(y, k, 0)
        top = jnp.where(row8 < k, pltpu.roll(prev, k, 0), r[:8])
        return jnp.concatenate([top, r[8:]], axis=0)

    cw = convw_ref[...]
    conv = shifted(2) * cw[0:1] + shifted(1) * cw[1:2] + y * cw[2:3]
    ybc_ref[:, SG_WIDTH:] = _rms(gate_b * conv, ong[:, SG_WIDTH:]).astype(jnp.bfloat16)


def _inproj(x, lw, tabs, seq):
    t = x.shape[0]
    tm = TM_IN
    tab, cos_t, sin_t = tabs
    row = lambda i: (i, 0)
    col = lambda i: (0, i)
    consts = [lw["g_pre"], lw["w_in"], lw["q_norm_g"], lw["w_uq_t"], lw["kv_norm_g"], lw["w_k"],
              lw["w_kr"], lw["w_v_t"]]
    consts2 = [lw["sg_ln_g"], lw["sg_ln_b"], lw["avg"], lw["w_sp"], lw["b_sp"], lw["conv_w"],
               lw["out_norm_g_bc"]]
    in_specs = ([pl.BlockSpec((tm, D_MODEL), row)]
                + [_const_spec(c.shape) for c in consts]
                + [pl.BlockSpec((tm, 2 * ROPE_DIM), row),
                   pl.BlockSpec((HALF_ROPE, tm), col),
                   pl.BlockSpec((HALF_ROPE, tm), col)]
                + [_const_spec(c.shape) for c in consts2])
    out_shape = [jax.ShapeDtypeStruct((QK_WIDTH, t), jnp.bfloat16),
                 jax.ShapeDtypeStruct((t, QK_WIDTH), jnp.bfloat16),
                 jax.ShapeDtypeStruct((t // T_ATT, MLA_WIDTH, T_ATT), jnp.bfloat16),
                 jax.ShapeDtypeStruct((t, SG_WIDTH + CV_WIDTH), jnp.bfloat16)]
    out_specs = [pl.BlockSpec((QK_WIDTH, tm), col),
                 pl.BlockSpec((tm, QK_WIDTH), row),
                 pl.BlockSpec((tm // T_ATT, MLA_WIDTH, T_ATT), lambda i: (i, 0, 0)),
                 pl.BlockSpec((tm, SG_WIDTH + CV_WIDTH), row)]
    return pl.pallas_call(
        functools.partial(_inproj_kernel, seq // tm),
        grid=(t // tm,),
        in_specs=in_specs,
        out_specs=out_specs,
        out_shape=out_shape,
        scratch_shapes=[pltpu.VMEM((8, CV_WIDTH), jnp.float32)],
        compiler_params=pltpu.CompilerParams(
            dimension_semantics=("arbitrary",), vmem_limit_bytes=VMEM_LIMIT_BYTES),
        name="inproj",
    )(x, *consts, tab, cos_t, sin_t, *consts2)


def _attn_kernel(qt_ref, k_ref, vt_ref, o_ref, acc_ref):
    tq = qt_ref.shape[1]
    i = pl.program_id(1)
    key_row = lax.broadcasted_iota(jnp.int32, (tq, tq), 0)
    qry_col = lax.broadcasted_iota(jnp.int32, (tq, tq), 1)
    causal = key_row <= qry_col

    for hd in range(N_HEADS):
        q_h = qt_ref[hd * HEAD_PAD:(hd + 1) * HEAD_PAD, :]

        def block(kb, carry, masked):
            m, l, acc = carry
            start = pl.multiple_of(kb * tq, tq)
            k_blk = k_ref[pl.ds(start, tq), hd * HEAD_PAD:(hd + 1) * HEAD_PAD]
            s = _dot(k_blk, q_h)
            if masked:
                s = jnp.where(causal, s, MASK_VALUE)
            m_new = jnp.maximum(m, jnp.max(s, axis=0, keepdims=True))
            alpha = jnp.exp2(m - m_new)
            p = jnp.exp2(s - m_new)
            l = alpha * l + jnp.sum(p, axis=0, keepdims=True)
            v_blk = vt_ref[kb, hd * V_DIM:(hd + 1) * V_DIM, :]
            acc = alpha * acc + _dot(v_blk, p.astype(jnp.bfloat16))
            return m_new, l, acc

        init = (jnp.full((1, tq), MASK_VALUE, jnp.float32),
                jnp.zeros((1, tq), jnp.float32),
                jnp.zeros((V_DIM, tq), jnp.float32))
        carry = lax.fori_loop(0, i, lambda kb, c: block(kb, c, False), init)
        m, l, acc = block(i, carry, True)
        acc_ref[hd * V_DIM:(hd + 1) * V_DIM, :] = acc / l

    o_ref[...] = acc_ref[...].T.astype(o_ref.dtype)


def _attention(qt, k, vt, batch, seq):
    t = k.shape[0]
    tq = T_ATT
    nq = seq // tq
    return pl.pallas_call(
        _attn_kernel,
        grid=(batch, nq),
        in_specs=[pl.BlockSpec((QK_WIDTH, tq), lambda b, i: (0, b * nq + i)),
                  pl.BlockSpec((seq, QK_WIDTH), lambda b, i: (b, 0)),
                  pl.BlockSpec((nq, MLA_WIDTH, tq), lambda b, i: (b, 0, 0))],
        out_specs=pl.BlockSpec((tq, MLA_WIDTH), lambda b, i: (b * nq + i, 0)),
        out_shape=jax.ShapeDtypeStruct((t, MLA_WIDTH), jnp.bfloat16),
        scratch_shapes=[pltpu.VMEM((MLA_WIDTH, tq), jnp.float32)],
        compiler_params=pltpu.CompilerParams(
            dimension_semantics=("arbitrary", "arbitrary"), vmem_limit_bytes=VMEM_LIMIT_BYTES),
        name="attention",
    )(qt, k, vt)


def _outffn_kernel(x_ref, ya_ref, ybc_ref, onga_ref, wout_ref, gpost_ref, gffn_ref,
                   wgate_ref, wup_ref, wdown_ref, gffn_post_ref, o_ref):
    ya = _rms(ya_ref[...].astype(jnp.float32), onga_ref[...]).astype(jnp.bfloat16)
    o = (_dot(ya, wout_ref[:MLA_WIDTH, :]) + _dot(ybc_ref[...], wout_ref[MLA_WIDTH:, :]))
    x1 = x_ref[...] + _rms(o, gpost_ref[...])
    h = _rms(x1, gffn_ref[...]).astype(jnp.bfloat16)
    d = jnp.zeros(x1.shape, jnp.float32)
    for start, size in FF_CHUNKS:
        gate = _dot(h, wgate_ref[:, start:start + size])
        up = _dot(h, wup_ref[:, start:start + size])
        f = (gate * (1.0 / (1.0 + jnp.exp(-gate))) * up).astype(jnp.bfloat16)
        d = d + _dot(f, wdown_ref[start:start + size, :])
    o_ref[...] = x1 + _rms(d, gffn_post_ref[...])


def _outffn(x, ya, ybc, lw):
    t = x.shape[0]
    tm = TM_OUT
    row = lambda i: (i, 0)
    consts = [lw["out_norm_g_a"], lw["w_out"], lw["g_post"], lw["g_ffn_pre"], lw["w_gate"],
              lw["w_up"], lw["w_down"], lw["g_ffn_post"]]
    return pl.pallas_call(
        _outffn_kernel,
        grid=(t // tm,),
        in_specs=[pl.BlockSpec((tm, D_MODEL), row),
                  pl.BlockSpec((tm, MLA_WIDTH), row),
                  pl.BlockSpec((tm, SG_WIDTH + CV_WIDTH), row)]
                 + [_const_spec(c.shape) for c in consts],
        out_specs=pl.BlockSpec((tm, D_MODEL), row),
        out_shape=jax.ShapeDtypeStruct((t, D_MODEL), jnp.float32),
        compiler_params=pltpu.CompilerParams(
            dimension_semantics=("arbitrary",), vmem_limit_bytes=VMEM_LIMIT_BYTES),
        name="outffn",
    )(x, ya, ybc, *consts)


def _prep_layer(l, p):
    bf = jnp.bfloat16
    f32 = jnp.float32
    w_in = p["w_in"][l]
    d = w_in.shape[0]
    kr = w_in[:, OFF_KR:OFF_SG]
    w_in_p = jnp.concatenate([
        w_in[:, OFF_CQ:OFF_KR],
        kr, kr, jnp.zeros((d, LANES - 2 * ROPE_DIM), f32),
        w_in[:, OFF_SG:IN_WIDTH]], axis=1).astype(bf)

    w_uq = p["w_uq"][l].reshape(Q_RANK, N_HEADS, NOPE_DIM + ROPE_DIM) * (SOFTMAX_SCALE * LOG2E)
    w_uq = jnp.pad(w_uq, ((0, 0), (0, 0), (0, HEAD_PAD - NOPE_DIM - ROPE_DIM)))
    w_uq_t = w_uq.reshape(Q_RANK, QK_WIDTH).T.astype(bf)

    w_ukv = p["w_ukv"][l].reshape(KV_RANK, N_HEADS, NOPE_DIM + V_DIM)
    w_k = jnp.pad(w_ukv[:, :, :NOPE_DIM], ((0, 0), (0, 0), (0, HEAD_PAD - NOPE_DIM)))
    w_k = w_k.reshape(KV_RANK, QK_WIDTH).astype(bf)
    w_v_t = w_ukv[:, :, NOPE_DIM:].reshape(KV_RANK, MLA_WIDTH).T.astype(bf)

    g = p["out_norm_g"][l]
    return {
        "g_pre": p["mix_pre_g"][l].reshape(1, -1),
        "w_in": w_in_p,
        "q_norm_g": p["q_norm_g"][l].reshape(1, -1),
        "w_uq_t": w_uq_t,
        "kv_norm_g": p["kv_norm_g"][l].reshape(1, -1),
        "w_k": w_k,
        "w_v_t": w_v_t,
        "sg_ln_g": p["sg_ln_g"][l].reshape(1, -1),
        "sg_ln_b": p["sg_ln_b"][l].reshape(1, -1),
        "w_sp": p["w_sp"][l].reshape(SG_GROUPS * CHUNK, CHUNK),
        "b_sp": jnp.repeat(p["b_sp"][l].T, HEAD_DIM, axis=1),
        "conv_w": p["conv_w"][l],
        "out_norm_g_a": g[:MLA_WIDTH].reshape(1, -1),
        "out_norm_g_bc": g[MLA_WIDTH:].reshape(1, -1),
        "w_out": p["w_out"][l].astype(bf),
        "g_post": p["mix_post_g"][l].reshape(1, -1),
        "g_ffn_pre": p["ffn_pre_g"][l].reshape(1, -1),
        "w_gate": p["w_gate"][l].astype(bf),
        "w_up": p["w_up"][l].astype(bf),
        "w_down": p["w_down"][l].astype(bf),
        "g_ffn_post": p["ffn_post_g"][l].reshape(1, -1),
    }


def _shared_consts():
    bf = jnp.bfloat16
    src = jnp.arange(2 * ROPE_DIM)
    feat = jnp.where(src < ROPE_DIM, src, (src - ROPE_DIM + HALF_ROPE) % ROPE_DIM)
    dst = jnp.arange(QK_WIDTH)
    w_kr = ((dst[None, :] % HEAD_PAD) == (NOPE_DIM + feat)[:, None]).astype(bf)
    grp = jnp.arange(SG_WIDTH) // HEAD_DIM
    avg = ((grp[:, None] == grp[None, :]).astype(jnp.float32) / HEAD_DIM).astype(bf)
    return w_kr, avg


def kernel(x, positions, mix_pre_g, mix_post_g, ffn_pre_g, ffn_post_g, w_in, q_norm_g, w_uq,
           kv_norm_g, w_ukv, sg_ln_g, sg_ln_b, w_sp, b_sp, conv_w, out_norm_g, w_out,
           w_gate, w_up, w_down):
    params = dict(mix_pre_g=mix_pre_g, mix_post_g=mix_post_g, ffn_pre_g=ffn_pre_g,
                  ffn_post_g=ffn_post_g, w_in=w_in, q_norm_g=q_norm_g, w_uq=w_uq,
                  kv_norm_g=kv_norm_g, w_ukv=w_ukv, sg_ln_g=sg_ln_g, sg_ln_b=sg_ln_b,
                  w_sp=w_sp, b_sp=b_sp, conv_w=conv_w, out_norm_g=out_norm_g, w_out=w_out,
                  w_gate=w_gate, w_up=w_up, w_down=w_down)
    batch, seq, d = x.shape
    depth = w_in.shape[0]
    assert seq % TM_IN == 0 and seq % T_ATT == 0 and TM_IN % T_ATT == 0
    assert (batch * seq) % TM_OUT == 0 and d == D_MODEL

    cos_t, sin_t = _rope_tables(positions)
    cos_r, sin_r = cos_t.T, sin_t.T
    tab = jnp.concatenate([cos_r, cos_r, sin_r, -sin_r], axis=1)
    w_kr, avg = _shared_consts()

    xf = x.reshape(batch * seq, d)
    for l in range(depth):
        lw = _prep_layer(l, params)
        lw["w_kr"] = w_kr
        lw["avg"] = avg
        qt, k, vt, ybc = _inproj(xf, lw, (tab, cos_t, sin_t), seq)
        ya = _attention(qt, k, vt, batch, seq)
        xf = _outffn(xf, ya, ybc, lw)
    return xf.reshape(batch, seq, d)
```

```python
import functools
import math

import jax
import jax.numpy as jnp
from jax import lax
from jax.experimental import pallas as pl
from jax.experimental.pallas import tpu as pltpu

D_MODEL = 1024
HEAD_DIM = 64
N_HEADS = 8
NOPE_DIM = 64
ROPE_DIM = 32
HALF_ROPE = ROPE_DIM // 2
V_DIM = 64
Q_RANK = 384
KV_RANK = 256
ROPE_THETA = 10000.0
SG_GROUPS = 4
SG_WIDTH = 256
CHUNK = 128
CV_WIDTH = 256
MLA_WIDTH = N_HEADS * V_DIM
D_FF = 2816
EPS = 1e-6

OFF_CQ = 0
OFF_CKV = OFF_CQ + Q_RANK
OFF_KR = OFF_CKV + KV_RANK
OFF_SG = OFF_KR + ROPE_DIM
OFF_CV = OFF_SG + 2 * SG_WIDTH
IN_WIDTH = OFF_CV + 3 * CV_WIDTH

P_CQ = 0
P_CKV = 384
P_KR = 640
P_SG = 768
P_CV = 1280
P_WIDTH = 2048

HEAD_PAD = 128
QK_WIDTH = N_HEADS * HEAD_PAD

VMEM_LIMIT_BYTES = 56 * 1024 * 1024
LANES = 128

TM_IN = 512
T_ATT = 256
TM_OUT = 512
FF_CHUNKS = ((0, 1024), (1024, 1024), (2048, 768))

_NT = (((1,), (1,)), ((), ()))
SOFTMAX_SCALE = (NOPE_DIM + ROPE_DIM) ** -0.5
LOG2E = 1.4426950408889634
MASK_VALUE = -1e30


def _dot(a, b):
    return jnp.dot(a, b, preferred_element_type=jnp.float32)


def _dot_nt(a, b):
    return lax.dot_general(a, b, _NT, preferred_element_type=jnp.float32)


def _rms(x, g):
    return x * lax.rsqrt(jnp.mean(x * x, axis=-1, keepdims=True) + EPS) * g


def _const_spec(shape):
    nd = len(shape)
    return pl.BlockSpec(shape, lambda *_: (0,) * nd, pipeline_mode=pl.Buffered(1))


def _rope_kernel(pos_ref, inv_ref, cos_ref, sin_ref):
    ang = pos_ref[...].astype(jnp.float32) * inv_ref[...]
    cos_ref[...] = jnp.cos(ang)
    sin_ref[...] = jnp.sin(ang)


def _rope_tables(positions):
    t = positions.size
    tn = 4096
    pos = positions.reshape(1, t)
    inv_freq = 1.0 / (ROPE_THETA ** (jnp.arange(0, HALF_ROPE, dtype=jnp.float32) / HALF_ROPE))
    return pl.pallas_call(
        _rope_kernel,
        grid=(t // tn,),
        in_specs=[pl.BlockSpec((1, tn), lambda i: (0, i)),
                  pl.BlockSpec((HALF_ROPE, 1), lambda i: (0, 0))],
        out_specs=[pl.BlockSpec((HALF_ROPE, tn), lambda i: (0, i))] * 2,
        out_shape=[jax.ShapeDtypeStruct((HALF_ROPE, t), jnp.float32)] * 2,
        name="rope_tables",
    )(pos, inv_freq.reshape(HALF_ROPE, 1))


def _inproj_kernel(steps_per_seq,
                   x_ref, g_ref, win_ref, qg_ref, wuqt_ref, kvg_ref, wk_ref, wkr_ref, wvt_ref,
                   tab_ref, cost_ref, sint_ref, lng_ref, lnb_ref, avg_ref, wsp_ref, bsp_ref,
                   convw_ref, ong_ref,
                   qt_ref, k_ref, vt_ref, ybc_ref, carry_ref):
    tm = x_ref.shape[0]
    h = _rms(x_ref[...], g_ref[...]).astype(jnp.bfloat16)
    z = _dot(h, win_ref[...])

    c_q = _rms(z[:, P_CQ:P_CQ + Q_RANK], qg_ref[...]).astype(jnp.bfloat16)
    qt = _dot_nt(wuqt_ref[...], c_q)
    cos_t = cost_ref[...]
    sin_t = sint_ref[...]
    for hd in range(N_HEADS):
        base = hd * HEAD_PAD
        t1 = qt[base + NOPE_DIM:base + NOPE_DIM + HALF_ROPE]
        t2 = qt[base + NOPE_DIM + HALF_ROPE:base + NOPE_DIM + ROPE_DIM]
        qt_ref[base:base + NOPE_DIM, :] = qt[base:base + NOPE_DIM].astype(jnp.bfloat16)
        qt_ref[base + NOPE_DIM:base + NOPE_DIM + HALF_ROPE, :] = (
            t1 * cos_t - t2 * sin_t).astype(jnp.bfloat16)
        qt_ref[base + NOPE_DIM + HALF_ROPE:base + NOPE_DIM + ROPE_DIM, :] = (
            t2 * cos_t + t1 * sin_t).astype(jnp.bfloat16)
        qt_ref[base + NOPE_DIM + ROPE_DIM:base + HEAD_PAD, :] = jnp.zeros(
            (HEAD_PAD - NOPE_DIM - ROPE_DIM, tm), jnp.bfloat16)

    c_kv = _rms(z[:, P_CKV:P_CKV + KV_RANK], kvg_ref[...]).astype(jnp.bfloat16)
    kr_ab = (z[:, P_KR:P_KR + 2 * ROPE_DIM] * tab_ref[...]).astype(jnp.bfloat16)
    k_ref[...] = (_dot(c_kv, wk_ref[...]) + _dot(kr_ab, wkr_ref[...])).astype(jnp.bfloat16)
    vt = _dot_nt(wvt_ref[...], c_kv).astype(jnp.bfloat16)
    tk = vt_ref.shape[2]
    for j in range(tm // tk):
        vt_ref[j] = vt[:, j * tk:(j + 1) * tk]

    uv = z[:, P_SG:P_SG + 2 * SG_WIDTH]
    uv = uv * (0.5 * (1.0 + jnp.tanh(math.sqrt(2.0 / math.pi) * (uv + 0.044715 * (uv * uv * uv)))))
    u = uv[:, :SG_WIDTH]
    v = uv[:, SG_WIDTH:]
    avg = avg_ref[...]

    def group_mean(a):
        hi = a.astype(jnp.bfloat16)
        lo = (a - hi.astype(jnp.float32)).astype(jnp.bfloat16)
        return _dot(hi, avg) + _dot(lo, avg)

    dv = v - group_mean(v)
    vn = dv * lax.rsqrt(group_mean(dv * dv) + EPS) * lng_ref[...] + lnb_ref[...]
    vn = vn.astype(jnp.bfloat16)
    rows = lax.broadcasted_iota(jnp.int32, (SG_GROUPS * CHUNK, CHUNK), 0)
    cols = lax.broadcasted_iota(jnp.int32, (SG_GROUPS * CHUNK, CHUNK), 1)
    w_causal = jnp.where((rows % CHUNK) >= cols, wsp_ref[...], 0.0).astype(jnp.bfloat16)
    lane_group = lax.broadcasted_iota(jnp.int32, (CHUNK, SG_WIDTH), 1) // HEAD_DIM
    bsp = bsp_ref[...]
    mixed = []
    for c in range(tm // CHUNK):
        o = _dot(w_causal, vn[c * CHUNK:(c + 1) * CHUNK])
        m = o[3 * CHUNK:]
        for g in (2, 1, 0):
            m = jnp.where(lane_group == g, o[g * CHUNK:(g + 1) * CHUNK], m)
        mixed.append(m + bsp)
    y_b = u * jnp.concatenate(mixed, axis=0)
    ong = ong_ref[...]
    ybc_ref[:, :SG_WIDTH] = _rms(y_b, ong[:, :SG_WIDTH]).astype(jnp.bfloat16)

    gate_b = z[:, P_CV:P_CV + CV_WIDTH]
    y = z[:, P_CV + CV_WIDTH:P_CV + 2 * CV_WIDTH] * z[:, P_CV + 2 * CV_WIDTH:P_CV + 3 * CV_WIDTH]

    @pl.when(pl.program_id(0) % steps_per_seq == 0)
    def _():
        carry_ref[...] = jnp.zeros_like(carry_ref)

    prev = carry_ref[...]
    carry_ref[...] = y[tm - 8:]
    row8 = lax.broadcasted_iota(jnp.int32, (8, CV_WIDTH), 0)

    def shifted(k):
        r = pltpu.roll(y, k, 0)
        top = jnp.where(row8 < k, pltpu.roll(prev, k, 0), r[:8])
        return jnp.concatenate([top, r[8:]], axis=0)

    cw = convw_ref[...]
    conv = shifted(2) * cw[0:1] + shifted(1) * cw[1:2] + y * cw[2:3]
    ybc_ref[:, SG_WIDTH:] = _rms(gate_b * conv, ong[:, SG_WIDTH:]).astype(jnp.bfloat16)


def _inproj(x, lw, tabs, seq):
    t = x.shape[0]
    tm = TM_IN
    tab, cos_t, sin_t = tabs
    row = lambda i: (i, 0)
    col = lambda i: (0, i)
    consts = [lw["g_pre"], lw["w_in"], lw["q_norm_g"], lw["w_uq_t"], lw["kv_norm_g"], lw["w_k"],
              lw["w_kr"], lw["w_v_t"]]
    consts2 = [lw["sg_ln_g"], lw["sg_ln_b"], lw["avg"], lw["w_sp"], lw["b_sp"], lw["conv_w"],
               lw["out_norm_g_bc"]]
    in_specs = ([pl.BlockSpec((tm, D_MODEL), row)]
                + [_const_spec(c.shape) for c in consts]
                + [pl.BlockSpec((tm, 2 * ROPE_DIM), row),
                   pl.BlockSpec((HALF_ROPE, tm), col),
                   pl.BlockSpec((HALF_ROPE, tm), col)]
                + [_const_spec(c.shape) for c in consts2])
    out_shape = [jax.ShapeDtypeStruct((QK_WIDTH, t), jnp.bfloat16),
                 jax.ShapeDtypeStruct((t, QK_WIDTH), jnp.bfloat16),
                 jax.ShapeDtypeStruct((t // T_ATT, MLA_WIDTH, T_ATT), jnp.bfloat16),
                 jax.ShapeDtypeStruct((t, SG_WIDTH + CV_WIDTH), jnp.bfloat16)]
    out_specs = [pl.BlockSpec((QK_WIDTH, tm), col),
                 pl.BlockSpec((tm, QK_WIDTH), row),
                 pl.BlockSpec((tm // T_ATT, MLA_WIDTH, T_ATT), lambda i: (i, 0, 0)),
                 pl.BlockSpec((tm, SG_WIDTH + CV_WIDTH), row)]
    return pl.pallas_call(
        functools.partial(_inproj_kernel, seq // tm),
        grid=(t // tm,),
        in_specs=in_specs,
        out_specs=out_specs,
        out_shape=out_shape,
        scratch_shapes=[pltpu.VMEM((8, CV_WIDTH), jnp.float32)],
        compiler_params=pltpu.CompilerParams(
            dimension_semantics=("arbitrary",), vmem_limit_bytes=VMEM_LIMIT_BYTES),
        name="inproj",
    )(x, *consts, tab, cos_t, sin_t, *consts2)


def _attn_kernel(qt_ref, k_ref, vt_ref, o_ref, m_ref, l_ref, acc_ref):
    tq = qt_ref.shape[1]
    i = pl.program_id(1)
    m_ref[...] = jnp.full(m_ref.shape, MASK_VALUE, jnp.float32)
    l_ref[...] = jnp.zeros(l_ref.shape, jnp.float32)
    acc_ref[...] = jnp.zeros(acc_ref.shape, jnp.float32)

    def step(kb, masked):
        start = pl.multiple_of(kb * tq, tq)
        scores = []
        for hd in range(N_HEADS):
            q_h = qt_ref[hd * HEAD_PAD:(hd + 1) * HEAD_PAD, :]
            k_blk = k_ref[pl.ds(start, tq), hd * HEAD_PAD:(hd + 1) * HEAD_PAD]
            scores.append(_dot(k_blk, q_h))
        probs = []
        for hd in range(N_HEADS):
            s = scores[hd]
            if masked:
                key_row = lax.broadcasted_iota(jnp.int32, (tq, tq), 0)
                qry_col = lax.broadcasted_iota(jnp.int32, (tq, tq), 1)
                s = jnp.where(key_row <= qry_col, s, MASK_VALUE)
            m = m_ref[hd:hd + 1, :]
            m_new = jnp.maximum(m, jnp.max(s, axis=0, keepdims=True))
            alpha = jnp.exp2(m - m_new)
            p = jnp.exp2(s - m_new)
            l_new = alpha * l_ref[hd:hd + 1, :] + jnp.sum(p, axis=0, keepdims=True)
            probs.append((m_new, l_new, alpha, p.astype(jnp.bfloat16)))
        new_acc = []
        for hd in range(N_HEADS):
            alpha, p = probs[hd][2], probs[hd][3]
            v_blk = vt_ref[kb, hd * V_DIM:(hd + 1) * V_DIM, :]
            slab = slice(hd * V_DIM, (hd + 1) * V_DIM)
            new_acc.append(alpha * acc_ref[slab, :] + _dot(v_blk, p))
        for hd in range(N_HEADS):
            m_ref[hd:hd + 1, :] = probs[hd][0]
            l_ref[hd:hd + 1, :] = probs[hd][1]
            acc_ref[hd * V_DIM:(hd + 1) * V_DIM, :] = new_acc[hd]

    def body(kb, carry):
        step(kb, False)
        return carry

    lax.fori_loop(0, i, body, 0)
    step(i, True)
    for hd in range(N_HEADS):
        slab = slice(hd * V_DIM, (hd + 1) * V_DIM)
        acc_ref[slab, :] = acc_ref[slab, :] / l_ref[hd:hd + 1, :]
    o_ref[...] = acc_ref[...].T.astype(o_ref.dtype)


def _attention(qt, k, vt, batch, seq):
    t = k.shape[0]
    tq = T_ATT
    nq = seq // tq
    return pl.pallas_call(
        _attn_kernel,
        grid=(batch, nq),
        in_specs=[pl.BlockSpec((QK_WIDTH, tq), lambda b, i: (0, b * nq + i)),
                  pl.BlockSpec((seq, QK_WIDTH), lambda b, i: (b, 0)),
                  pl.BlockSpec((nq, MLA_WIDTH, tq), lambda b, i: (b, 0, 0))],
        out_specs=pl.BlockSpec((tq, MLA_WIDTH), lambda b, i: (b * nq + i, 0)),
        out_shape=jax.ShapeDtypeStruct((t, MLA_WIDTH), jnp.bfloat16),
        scratch_shapes=[pltpu.VMEM((N_HEADS, tq), jnp.float32),
                        pltpu.VMEM((N_HEADS, tq), jnp.float32),
                        pltpu.VMEM((MLA_WIDTH, tq), jnp.float32)],
        compiler_params=pltpu.CompilerParams(
            dimension_semantics=("arbitrary", "arbitrary"), vmem_limit_bytes=VMEM_LIMIT_BYTES),
        name="attention",
    )(qt, k, vt)


def _outffn_kernel(x_ref, ya_ref, ybc_ref, onga_ref, wout_ref, gpost_ref, gffn_ref,
                   wgate_ref, wup_ref, wdown_ref, gffn_post_ref, o_ref):
    ya = _rms(ya_ref[...].astype(jnp.float32), onga_ref[...]).astype(jnp.bfloat16)
    o = (_dot(ya, wout_ref[:MLA_WIDTH, :]) + _dot(ybc_ref[...], wout_ref[MLA_WIDTH:, :]))
    x1 = x_ref[...] + _rms(o, gpost_ref[...])
    h = _rms(x1, gffn_ref[...]).astype(jnp.bfloat16)
    d = jnp.zeros(x1.shape, jnp.float32)
    for start, size in FF_CHUNKS:
        gate = _dot(h, wgate_ref[:, start:start + size])
        up = _dot(h, wup_ref[:, start:start + size])
        f = (gate * (1.0 / (1.0 + jnp.exp(-gate))) * up).astype(jnp.bfloat16)
        d = d + _dot(f, wdown_ref[start:start + size, :])
    o_ref[...] = x1 + _rms(d, gffn_post_ref[...])


def _outffn(x, ya, ybc, lw):
    t = x.shape[0]
    tm = TM_OUT
    row = lambda i: (i, 0)
    consts = [lw["out_norm_g_a"], lw["w_out"], lw["g_post"], lw["g_ffn_pre"], lw["w_gate"],
              lw["w_up"], lw["w_down"], lw["g_ffn_post"]]
    return pl.pallas_call(
        _outffn_kernel,
        grid=(t // tm,),
        in_specs=[pl.BlockSpec((tm, D_MODEL), row),
                  pl.BlockSpec((tm, MLA_WIDTH), row),
                  pl.BlockSpec((tm, SG_WIDTH + CV_WIDTH), row)]
                 + [_const_spec(c.shape) for c in consts],
        out_specs=pl.BlockSpec((tm, D_MODEL), row),
        out_shape=jax.ShapeDtypeStruct((t, D_MODEL), jnp.float32),
        compiler_params=pltpu.CompilerParams(
            dimension_semantics=("arbitrary",), vmem_limit_bytes=VMEM_LIMIT_BYTES),
        name="outffn",
    )(x, ya, ybc, *consts)


def _prep_layer(l, p):
    bf = jnp.bfloat16
    f32 = jnp.float32
    w_in = p["w_in"][l]
    d = w_in.shape[0]
    kr = w_in[:, OFF_KR:OFF_SG]
    w_in_p = jnp.concatenate([
        w_in[:, OFF_CQ:OFF_KR],
        kr, kr, jnp.zeros((d, LANES - 2 * ROPE_DIM), f32),
        w_in[:, OFF_SG:IN_WIDTH]], axis=1).astype(bf)

    w_uq = p["w_uq"][l].reshape(Q_RANK, N_HEADS, NOPE_DIM + ROPE_DIM) * (SOFTMAX_SCALE * LOG2E)
    w_uq = jnp.pad(w_uq, ((0, 0), (0, 0), (0, HEAD_PAD - NOPE_DIM - ROPE_DIM)))
    w_uq_t = w_uq.reshape(Q_RANK, QK_WIDTH).T.astype(bf)

    w_ukv = p["w_ukv"][l].reshape(KV_RANK, N_HEADS, NOPE_DIM + V_DIM)
    w_k = jnp.pad(w_ukv[:, :, :NOPE_DIM], ((0, 0), (0, 0), (0, HEAD_PAD - NOPE_DIM)))
    w_k = w_k.reshape(KV_RANK, QK_WIDTH).astype(bf)
    w_v_t = w_ukv[:, :, NOPE_DIM:].reshape(KV_RANK, MLA_WIDTH).T.astype(bf)

    g = p["out_norm_g"][l]
    return {
        "g_pre": p["mix_pre_g"][l].reshape(1, -1),
        "w_in": w_in_p,
        "q_norm_g": p["q_norm_g"][l].reshape(1, -1),
        "w_uq_t": w_uq_t,
        "kv_norm_g": p["kv_norm_g"][l].reshape(1, -1),
        "w_k": w_k,
        "w_v_t": w_v_t,
        "sg_ln_g": p["sg_ln_g"][l].reshape(1, -1),
        "sg_ln_b": p["sg_ln_b"][l].reshape(1, -1),
        "w_sp": p["w_sp"][l].reshape(SG_GROUPS * CHUNK, CHUNK),
        "b_sp": jnp.repeat(p["b_sp"][l].T, HEAD_DIM, axis=1),
        "conv_w": p["conv_w"][l],
        "out_norm_g_a": g[:MLA_WIDTH].reshape(1, -1),
        "out_norm_g_bc": g[MLA_WIDTH:].reshape(1, -1),
        "w_out": p["w_out"][l].astype(bf),
        "g_post": p["mix_post_g"][l].reshape(1, -1),
        "g_ffn_pre": p["ffn_pre_g"][l].reshape(1, -1),
        "w_gate": p["w_gate"][l].astype(bf),
        "w_up": p["w_up"][l].astype(bf),
        "w_down": p["w_down"][l].astype(bf),
        "g_ffn_post": p["ffn_post_g"][l].reshape(1, -1),
    }


def _shared_consts():
    bf = jnp.bfloat16
    src = jnp.arange(2 * ROPE_DIM)
    feat = jnp.where(src < ROPE_DIM, src, (src - ROPE_DIM + HALF_ROPE) % ROPE_DIM)
    dst = jnp.arange(QK_WIDTH)
    w_kr = ((dst[None, :] % HEAD_PAD) == (NOPE_DIM + feat)[:, None]).astype(bf)
    grp = jnp.arange(SG_WIDTH) // HEAD_DIM
    avg = ((grp[:, None] == grp[None, :]).astype(jnp.float32) / HEAD_DIM).astype(bf)
    return w_kr, avg


def kernel(x, positions, mix_pre_g, mix_post_g, ffn_pre_g, ffn_post_g, w_in, q_norm_g, w_uq,
           kv_norm_g, w_ukv, sg_ln_g, sg_ln_b, w_sp, b_sp, conv_w, out_norm_g, w_out,
           w_gate, w_up, w_down):
    params = dict(mix_pre_g=mix_pre_g, mix_post_g=mix_post_g, ffn_pre_g=ffn_pre_g,
                  ffn_post_g=ffn_post_g, w_in=w_in, q_norm_g=q_norm_g, w_uq=w_uq,
                  kv_norm_g=kv_norm_g, w_ukv=w_ukv, sg_ln_g=sg_ln_g, sg_ln_b=sg_ln_b,
                  w_sp=w_sp, b_sp=b_sp, conv_w=conv_w, out_norm_g=out_norm_g, w_out=w_out,
                  w_gate=w_gate, w_up=w_up, w_down=w_down)
    batch, seq, d = x.shape
    depth = w_in.shape[0]
    assert seq % TM_IN == 0 and seq % T_ATT == 0 and TM_IN % T_ATT == 0
    assert (batch * seq) % TM_OUT == 0 and d == D_MODEL

    cos_t, sin_t = _rope_tables(positions)
    cos_r, sin_r = cos_t.T, sin_t.T
    tab = jnp.concatenate([cos_r, cos_r, sin_r, -sin_r], axis=1)
    w_kr, avg = _shared_consts()

    xf = x.reshape(batch * seq, d)
    for l in range(depth):
        lw = _prep_layer(l, params)
        lw["w_kr"] = w_kr
        lw["avg"] = avg
        qt, k, vt, ybc = _inproj(xf, lw, (tab, cos_t, sin_t), seq)
        ya = _attention(qt, k, vt, batch, seq)
        xf = _outffn(xf, ya, ybc, lw)
    return xf.reshape(batch, seq, d)
```

```python
import functools
import math

import jax
import jax.numpy as jnp
from jax import lax
from jax.experimental import pallas as pl
from jax.experimental.pallas import tpu as pltpu

D_MODEL = 1024
HEAD_DIM = 64
N_HEADS = 8
NOPE_DIM = 64
ROPE_DIM = 32
HALF_ROPE = ROPE_DIM // 2
V_DIM = 64
Q_RANK = 384
KV_RANK = 256
ROPE_THETA = 10000.0
SG_GROUPS = 4
SG_WIDTH = 256
CHUNK = 128
CV_WIDTH = 256
MLA_WIDTH = N_HEADS * V_DIM
D_FF = 2816
EPS = 1e-6

OFF_CQ = 0
OFF_CKV = OFF_CQ + Q_RANK
OFF_KR = OFF_CKV + KV_RANK
OFF_SG = OFF_KR + ROPE_DIM
OFF_CV = OFF_SG + 2 * SG_WIDTH
IN_WIDTH = OFF_CV + 3 * CV_WIDTH

P_CQ = 0
P_SG = 384
P_CKV = 896
P_KR = 1152
P_CV = 1280
P_WIDTH = 2048

HEAD_PAD = 128
QK_WIDTH = N_HEADS * HEAD_PAD
BF16_ROWS = 16
V_SLOT = V_DIM + BF16_ROWS
VT_ROWS = N_HEADS * V_SLOT

VMEM_LIMIT_BYTES = 56 * 1024 * 1024
LANES = 128

TM_IN = 512
T_ATT = 256
TM_OUT = 512
FF_CHUNKS = ((0, 1024), (1024, 1024), (2048, 768))

_NT = (((1,), (1,)), ((), ()))
SOFTMAX_SCALE = (NOPE_DIM + ROPE_DIM) ** -0.5
LOG2E = 1.4426950408889634
MASK_VALUE = -1e30


def _dot(a, b):
    return jnp.dot(a, b, preferred_element_type=jnp.float32)


def _dot_nt(a, b):
    return lax.dot_general(a, b, _NT, preferred_element_type=jnp.float32)


def _rms(x, g):
    return x * lax.rsqrt(jnp.mean(x * x, axis=-1, keepdims=True) + EPS) * g


def _const_spec(shape):
    nd = len(shape)
    return pl.BlockSpec(shape, lambda *_: (0,) * nd, pipeline_mode=pl.Buffered(1))


def _rope_kernel(pos_ref, inv_ref, cos_ref, sin_ref):
    ang = pos_ref[...].astype(jnp.float32) * inv_ref[...]
    cos_ref[...] = jnp.cos(ang)
    sin_ref[...] = jnp.sin(ang)


def _rope_tables(positions):
    t = positions.size
    tn = 4096
    pos = positions.reshape(1, t)
    inv_freq = 1.0 / (ROPE_THETA ** (jnp.arange(0, HALF_ROPE, dtype=jnp.float32) / HALF_ROPE))
    return pl.pallas_call(
        _rope_kernel,
        grid=(t // tn,),
        in_specs=[pl.BlockSpec((1, tn), lambda i: (0, i)),
                  pl.BlockSpec((HALF_ROPE, 1), lambda i: (0, 0))],
        out_specs=[pl.BlockSpec((HALF_ROPE, tn), lambda i: (0, i))] * 2,
        out_shape=[jax.ShapeDtypeStruct((HALF_ROPE, t), jnp.float32)] * 2,
        name="rope_tables",
    )(pos, inv_freq.reshape(HALF_ROPE, 1))


def _inproj_kernel(steps_per_seq,
                   x_ref, g_ref, win_ref, qg_ref, wuqt_ref, kvg_ref, wk_ref, wvt_ref,
                   ropec_ref, ropes_ref, cost_ref, sint_ref, lng_ref, lnb_ref, avg_ref, wsp_ref,
                   bsp_ref,
                   convw_ref, ong_ref,
                   qt_ref, k_ref, vt_ref, ybc_ref, carry_ref):
    tm = x_ref.shape[0]

    @pl.when(pl.program_id(0) % steps_per_seq == 0)
    def _():
        carry_ref[...] = jnp.zeros_like(carry_ref)

    h = _rms(x_ref[...], g_ref[...]).astype(jnp.bfloat16)
    z = _dot(h, win_ref[...])

    c_q = _rms(z[:, P_CQ:P_CQ + Q_RANK], qg_ref[...]).astype(jnp.bfloat16)
    qt = _dot_nt(wuqt_ref[...], c_q)
    cos_t = cost_ref[...]
    sin_t = sint_ref[...]
    for hd in range(N_HEADS):
        base = hd * HEAD_PAD
        t1 = qt[base + NOPE_DIM:base + NOPE_DIM + HALF_ROPE]
        t2 = qt[base + NOPE_DIM + HALF_ROPE:base + NOPE_DIM + ROPE_DIM]
        qt_ref[base:base + NOPE_DIM, :] = qt[base:base + NOPE_DIM].astype(jnp.bfloat16)
        qt_ref[base + NOPE_DIM:base + NOPE_DIM + HALF_ROPE, :] = (
            t1 * cos_t - t2 * sin_t).astype(jnp.bfloat16)
        qt_ref[base + NOPE_DIM + HALF_ROPE:base + NOPE_DIM + ROPE_DIM, :] = (
            t2 * cos_t + t1 * sin_t).astype(jnp.bfloat16)
        qt_ref[base + NOPE_DIM + ROPE_DIM:base + HEAD_PAD, :] = jnp.zeros(
            (HEAD_PAD - NOPE_DIM - ROPE_DIM, tm), jnp.bfloat16)

    uv = z[:, P_SG:P_SG + 2 * SG_WIDTH]
    uv = uv * (0.5 * (1.0 + jnp.tanh(math.sqrt(2.0 / math.pi) * (uv + 0.044715 * (uv * uv * uv)))))
    u = uv[:, :SG_WIDTH]
    v = uv[:, SG_WIDTH:]
    avg = avg_ref[...]

    def group_mean(a):
        hi = a.astype(jnp.bfloat16)
        lo = (a - hi.astype(jnp.float32)).astype(jnp.bfloat16)
        return _dot(hi, avg) + _dot(lo, avg)

    dv = v - group_mean(v)

    c_kv = _rms(z[:, P_CKV:P_CKV + KV_RANK], kvg_ref[...]).astype(jnp.bfloat16)
    zkr = z[:, P_KR:P_KR + LANES]
    k_rope = zkr * ropec_ref[...] + pltpu.roll(zkr, HALF_ROPE, 1) * ropes_ref[...]
    k_nope = _dot(c_kv, wk_ref[...])
    for hd in range(N_HEADS):
        slot = slice(hd * HEAD_PAD, (hd + 1) * HEAD_PAD)
        k_ref[:, slot] = (k_nope[:, slot] + k_rope).astype(jnp.bfloat16)

    vn = dv * lax.rsqrt(group_mean(dv * dv) + EPS) * lng_ref[...] + lnb_ref[...]
    vn = vn.astype(jnp.bfloat16)

    gate_b = z[:, P_CV:P_CV + CV_WIDTH]
    y = z[:, P_CV + CV_WIDTH:P_CV + 2 * CV_WIDTH] * z[:, P_CV + 2 * CV_WIDTH:P_CV + 3 * CV_WIDTH]
    prev = carry_ref[...]
    carry_ref[...] = y[tm - 8:]
    row8 = lax.broadcasted_iota(jnp.int32, (8, CV_WIDTH), 0)

    def shifted(k):
        r = pltpu.roll(y, k, 0)
        top = jnp.where(row8 < k, pltpu.roll(prev, k, 0), r[:8])
        return jnp.concatenate([top, r[8:]], axis=0)

    cw = convw_ref[...]
    conv = shifted(2) * cw[0:1] + shifted(1) * cw[1:2] + y * cw[2:3]
    ong = ong_ref[...]
    ybc_ref[:, SG_WIDTH:] = _rms(gate_b * conv, ong[:, SG_WIDTH:]).astype(jnp.bfloat16)

    rows = lax.broadcasted_iota(jnp.int32, (SG_GROUPS * CHUNK, CHUNK), 0)
    cols = lax.broadcasted_iota(jnp.int32, (SG_GROUPS * CHUNK, CHUNK), 1)
    w_causal = jnp.where((rows % CHUNK) >= cols, wsp_ref[...], 0.0).astype(jnp.bfloat16)
    lane_group = lax.broadcasted_iota(jnp.int32, (CHUNK, SG_WIDTH), 1) // HEAD_DIM
    bsp = bsp_ref[...]
    mixed = []
    for c in range(tm // CHUNK):
        o = _dot(w_causal, vn[c * CHUNK:(c + 1) * CHUNK])
        m = o[3 * CHUNK:]
        for g in (2, 1, 0):
            m = jnp.where(lane_group == g, o[g * CHUNK:(g + 1) * CHUNK], m)
        mixed.append(m + bsp)

    vt = _dot_nt(wvt_ref[...], c_kv).astype(jnp.bfloat16)
    tk = vt_ref.shape[2]
    ones = jnp.ones((BF16_ROWS, tk), jnp.bfloat16)
    for j in range(tm // tk):
        for hd in range(N_HEADS):
            vt_ref[j, hd * V_SLOT:hd * V_SLOT + V_DIM, :] = vt[hd * V_DIM:(hd + 1) * V_DIM,
                                                               j * tk:(j + 1) * tk]
            vt_ref[j, hd * V_SLOT + V_DIM:(hd + 1) * V_SLOT, :] = ones

    y_b = u * jnp.concatenate(mixed, axis=0)
    ybc_ref[:, :SG_WIDTH] = _rms(y_b, ong[:, :SG_WIDTH]).astype(jnp.bfloat16)


def _inproj(x, lw, tabs, seq):
    t = x.shape[0]
    tm = TM_IN
    rope_c, rope_s, cos_t, sin_t = tabs
    row = lambda i: (i, 0)
    col = lambda i: (0, i)
    consts = [lw["g_pre"], lw["w_in"], lw["q_norm_g"], lw["w_uq_t"], lw["kv_norm_g"], lw["w_k"],
              lw["w_v_t"]]
    consts2 = [lw["sg_ln_g"], lw["sg_ln_b"], lw["avg"], lw["w_sp"], lw["b_sp"], lw["conv_w"],
               lw["out_norm_g_bc"]]
    in_specs = ([pl.BlockSpec((tm, D_MODEL), row)]
                + [_const_spec(c.shape) for c in consts]
                + [pl.BlockSpec((tm, LANES), row),
                   pl.BlockSpec((tm, LANES), row),
                   pl.BlockSpec((HALF_ROPE, tm), col),
                   pl.BlockSpec((HALF_ROPE, tm), col)]
                + [_const_spec(c.shape) for c in consts2])
    out_shape = [jax.ShapeDtypeStruct((QK_WIDTH, t), jnp.bfloat16),
                 jax.ShapeDtypeStruct((t, QK_WIDTH), jnp.bfloat16),
                 jax.ShapeDtypeStruct((t // T_ATT, VT_ROWS, T_ATT), jnp.bfloat16),
                 jax.ShapeDtypeStruct((t, SG_WIDTH + CV_WIDTH), jnp.bfloat16)]
    out_specs = [pl.BlockSpec((QK_WIDTH, tm), col),
                 pl.BlockSpec((tm, QK_WIDTH), row),
                 pl.BlockSpec((tm // T_ATT, VT_ROWS, T_ATT), lambda i: (i, 0, 0)),
                 pl.BlockSpec((tm, SG_WIDTH + CV_WIDTH), row)]
    return pl.pallas_call(
        functools.partial(_inproj_kernel, seq // tm),
        grid=(t // tm,),
        in_specs=in_specs,
        out_specs=out_specs,
        out_shape=out_shape,
        scratch_shapes=[pltpu.VMEM((8, CV_WIDTH), jnp.float32)],
        compiler_params=pltpu.CompilerParams(
            dimension_semantics=("arbitrary",), vmem_limit_bytes=VMEM_LIMIT_BYTES),
        name="inproj",
    )(x, *consts, rope_c, rope_s, cos_t, sin_t, *consts2)


def _attn_kernel(qt_ref, k_ref, vt_ref, o_ref, m_ref, acc_ref, sa_ref, sb_ref):
    tq = qt_ref.shape[1]
    i = pl.program_id(1)
    m_ref[...] = jnp.full(m_ref.shape, MASK_VALUE, jnp.float32)
    acc_ref[...] = jnp.zeros(acc_ref.shape, jnp.float32)

    def scores(kb, dst_ref):
        start = pl.multiple_of(kb * tq, tq)
        for hd in range(N_HEADS):
            q_h = qt_ref[hd * HEAD_PAD:(hd + 1) * HEAD_PAD, :]
            k_blk = k_ref[pl.ds(start, tq), hd * HEAD_PAD:(hd + 1) * HEAD_PAD]
            dst_ref[hd] = _dot(k_blk, q_h)

    def consume(kb, src_ref, masked):
        if masked:
            key_row = lax.broadcasted_iota(jnp.int32, (tq, tq), 0)
            qry_col = lax.broadcasted_iota(jnp.int32, (tq, tq), 1)
            causal = key_row <= qry_col
        m_all = []
        for hd in range(N_HEADS):
            s = src_ref[hd]
            if masked:
                s = jnp.where(causal, s, MASK_VALUE)
            m = m_ref[hd:hd + 1, :]
            m_new = jnp.maximum(m, jnp.max(s, axis=0, keepdims=True))
            alpha = jnp.exp2(m - m_new)
            p = jnp.exp2(s - m_new).astype(jnp.bfloat16)
            m_all.append(m_new)
            slab = slice(hd * V_SLOT, (hd + 1) * V_SLOT)
            v_blk = vt_ref[kb, slab, :]
            acc_ref[slab, :] = alpha * acc_ref[slab, :] + _dot(v_blk, p)
        for hd in range(N_HEADS):
            m_ref[hd:hd + 1, :] = m_all[hd]

    scores(0, sa_ref)

    def pair(t, carry):
        kb = 2 * t
        scores(kb + 1, sb_ref)
        consume(kb, sa_ref, False)
        scores(kb + 2, sa_ref)
        consume(kb + 1, sb_ref, False)
        return carry

    def quad(t, carry):
        pair(2 * t, carry)
        return pair(2 * t + 1, carry)

    lax.fori_loop(0, i // 4, quad, 0)
    lax.fori_loop(2 * (i // 4), i // 2, pair, 0)

    @pl.when(i % 2 == 0)
    def _():
        consume(i, sa_ref, True)

    @pl.when(i % 2 == 1)
    def _():
        scores(i, sb_ref)
        consume(i - 1, sa_ref, False)
        consume(i, sb_ref, True)

    outs = []
    for hd in range(N_HEADS):
        base = hd * V_SLOT
        outs.append(acc_ref[base:base + V_DIM, :] / acc_ref[base + V_DIM:base + V_DIM + 1, :])
    o_ref[...] = jnp.concatenate(outs, axis=0).T.astype(o_ref.dtype)


def _attention(qt, k, vt, batch, seq):
    t = k.shape[0]
    tq = T_ATT
    nq = seq // tq
    return pl.pallas_call(
        _attn_kernel,
        grid=(batch, nq),
        in_specs=[pl.BlockSpec((QK_WIDTH, tq), lambda b, i: (0, b * nq + i)),
                  pl.BlockSpec((seq, QK_WIDTH), lambda b, i: (b, 0)),
                  pl.BlockSpec((nq, VT_ROWS, tq), lambda b, i: (b, 0, 0))],
        out_specs=pl.BlockSpec((tq, MLA_WIDTH), lambda b, i: (b * nq + i, 0)),
        out_shape=jax.ShapeDtypeStruct((t, MLA_WIDTH), jnp.bfloat16),
        scratch_shapes=[pltpu.VMEM((N_HEADS, tq), jnp.float32),
                        pltpu.VMEM((VT_ROWS, tq), jnp.float32),
                        pltpu.VMEM((N_HEADS, tq, tq), jnp.float32),
                        pltpu.VMEM((N_HEADS, tq, tq), jnp.float32)],
        compiler_params=pltpu.CompilerParams(
            dimension_semantics=("arbitrary", "arbitrary"), vmem_limit_bytes=VMEM_LIMIT_BYTES),
        name="attention",
    )(qt, k, vt)


def _outffn_kernel(x_ref, ya_ref, ybc_ref, onga_ref, wout_ref, gpost_ref, gffn_ref,
                   wgate_ref, wup_ref, wdown_ref, gffn_post_ref, o_ref):
    tm = x_ref.shape[0]
    halves = (slice(0, tm // 2), slice(tm // 2, tm))
    x1, h = [], []
    for rows in halves:
        ya = _rms(ya_ref[rows, :].astype(jnp.float32), onga_ref[...]).astype(jnp.bfloat16)
        o = _dot(ya, wout_ref[:MLA_WIDTH, :]) + _dot(ybc_ref[rows, :], wout_ref[MLA_WIDTH:, :])
        x1.append(x_ref[rows, :] + _rms(o, gpost_ref[...]))
    for k in range(2):
        h.append(_rms(x1[k], gffn_ref[...]).astype(jnp.bfloat16))
    d = [jnp.zeros(x1[0].shape, jnp.float32)] * 2
    for start, size in FF_CHUNKS:
        gate_up = [(_dot(h[k], wgate_ref[:, start:start + size]),
                    _dot(h[k], wup_ref[:, start:start + size])) for k in range(2)]
        for k in range(2):
            gate, up = gate_up[k]
            f = (gate * (1.0 / (1.0 + jnp.exp(-gate))) * up).astype(jnp.bfloat16)
            d[k] = d[k] + _dot(f, wdown_ref[start:start + size, :])
    for k, rows in enumerate(halves):
        o_ref[rows, :] = x1[k] + _rms(d[k], gffn_post_ref[...])


def _outffn(x, ya, ybc, lw):
    t = x.shape[0]
    tm = TM_OUT
    row = lambda i: (i, 0)
    consts = [lw["out_norm_g_a"], lw["w_out"], lw["g_post"], lw["g_ffn_pre"], lw["w_gate"],
              lw["w_up"], lw["w_down"], lw["g_ffn_post"]]
    return pl.pallas_call(
        _outffn_kernel,
        grid=(t // tm,),
        in_specs=[pl.BlockSpec((tm, D_MODEL), row),
                  pl.BlockSpec((tm, MLA_WIDTH), row),
                  pl.BlockSpec((tm, SG_WIDTH + CV_WIDTH), row)]
                 + [_const_spec(c.shape) for c in consts],
        out_specs=pl.BlockSpec((tm, D_MODEL), row),
        out_shape=jax.ShapeDtypeStruct((t, D_MODEL), jnp.float32),
        compiler_params=pltpu.CompilerParams(
            dimension_semantics=("arbitrary",), vmem_limit_bytes=VMEM_LIMIT_BYTES),
        name="outffn",
    )(x, ya, ybc, *consts)


def _prep_layer(l, p):
    bf = jnp.bfloat16
    f32 = jnp.float32
    w_in = p["w_in"][l]
    d = w_in.shape[0]
    t1 = w_in[:, OFF_KR:OFF_KR + HALF_ROPE]
    t2 = w_in[:, OFF_KR + HALF_ROPE:OFF_SG]
    w_in_p = jnp.concatenate([
        w_in[:, OFF_CQ:OFF_CKV],
        w_in[:, OFF_SG:OFF_CV],
        w_in[:, OFF_CKV:OFF_KR],
        jnp.zeros((d, NOPE_DIM - HALF_ROPE), f32), t2, t1, t2,
        jnp.zeros((d, HEAD_PAD - NOPE_DIM - ROPE_DIM), f32),
        w_in[:, OFF_CV:IN_WIDTH]], axis=1).astype(bf)

    w_uq = p["w_uq"][l].reshape(Q_RANK, N_HEADS, NOPE_DIM + ROPE_DIM) * (SOFTMAX_SCALE * LOG2E)
    w_uq = jnp.pad(w_uq, ((0, 0), (0, 0), (0, HEAD_PAD - NOPE_DIM - ROPE_DIM)))
    w_uq_t = w_uq.reshape(Q_RANK, QK_WIDTH).T.astype(bf)

    w_ukv = p["w_ukv"][l].reshape(KV_RANK, N_HEADS, NOPE_DIM + V_DIM)
    w_k = jnp.pad(w_ukv[:, :, :NOPE_DIM], ((0, 0), (0, 0), (0, HEAD_PAD - NOPE_DIM)))
    w_k = w_k.reshape(KV_RANK, QK_WIDTH).astype(bf)
    w_v_t = w_ukv[:, :, NOPE_DIM:].reshape(KV_RANK, MLA_WIDTH).T.astype(bf)

    g = p["out_norm_g"][l]
    return {
        "g_pre": p["mix_pre_g"][l].reshape(1, -1),
        "w_in": w_in_p,
        "q_norm_g": p["q_norm_g"][l].reshape(1, -1),
        "w_uq_t": w_uq_t,
        "kv_norm_g": p["kv_norm_g"][l].reshape(1, -1),
        "w_k": w_k,
        "w_v_t": w_v_t,
        "sg_ln_g": p["sg_ln_g"][l].reshape(1, -1),
        "sg_ln_b": p["sg_ln_b"][l].reshape(1, -1),
        "w_sp": p["w_sp"][l].reshape(SG_GROUPS * CHUNK, CHUNK),
        "b_sp": jnp.repeat(p["b_sp"][l].T, HEAD_DIM, axis=1),
        "conv_w": p["conv_w"][l],
        "out_norm_g_a": g[:MLA_WIDTH].reshape(1, -1),
        "out_norm_g_bc": g[MLA_WIDTH:].reshape(1, -1),
        "w_out": p["w_out"][l].astype(bf),
        "g_post": p["mix_post_g"][l].reshape(1, -1),
        "g_ffn_pre": p["ffn_pre_g"][l].reshape(1, -1),
        "w_gate": p["w_gate"][l].astype(bf),
        "w_up": p["w_up"][l].astype(bf),
        "w_down": p["w_down"][l].astype(bf),
        "g_ffn_post": p["ffn_post_g"][l].reshape(1, -1),
    }


def _group_average_matrix():
    grp = jnp.arange(SG_WIDTH) // HEAD_DIM
    return ((grp[:, None] == grp[None, :]).astype(jnp.float32) / HEAD_DIM).astype(jnp.bfloat16)


def _key_rope_tables(cos_r, sin_r):
    t = cos_r.shape[0]
    left = jnp.zeros((t, NOPE_DIM), jnp.float32)
    right = jnp.zeros((t, HEAD_PAD - NOPE_DIM - ROPE_DIM), jnp.float32)
    rope_c = jnp.concatenate([left, cos_r, cos_r, right], axis=1)
    rope_s = jnp.concatenate([left, -sin_r, sin_r, right], axis=1)
    return rope_c, rope_s


def kernel(x, positions, mix_pre_g, mix_post_g, ffn_pre_g, ffn_post_g, w_in, q_norm_g, w_uq,
           kv_norm_g, w_ukv, sg_ln_g, sg_ln_b, w_sp, b_sp, conv_w, out_norm_g, w_out,
           w_gate, w_up, w_down):
    params = dict(mix_pre_g=mix_pre_g, mix_post_g=mix_post_g, ffn_pre_g=ffn_pre_g,
                  ffn_post_g=ffn_post_g, w_in=w_in, q_norm_g=q_norm_g, w_uq=w_uq,
                  kv_norm_g=kv_norm_g, w_ukv=w_ukv, sg_ln_g=sg_ln_g, sg_ln_b=sg_ln_b,
                  w_sp=w_sp, b_sp=b_sp, conv_w=conv_w, out_norm_g=out_norm_g, w_out=w_out,
                  w_gate=w_gate, w_up=w_up, w_down=w_down)
    batch, seq, d = x.shape
    depth = w_in.shape[0]
    assert seq % TM_IN == 0 and seq % T_ATT == 0 and TM_IN % T_ATT == 0
    assert (batch * seq) % TM_OUT == 0 and d == D_MODEL

    cos_t, sin_t = _rope_tables(positions)
    rope_c, rope_s = _key_rope_tables(cos_t.T, sin_t.T)
    avg = _group_average_matrix()

    xf = x.reshape(batch * seq, d)
    for l in range(depth):
        lw = _prep_layer(l, params)
        lw["avg"] = avg
        qt, k, vt, ybc = _inproj(xf, lw, (rope_c, rope_s, cos_t, sin_t), seq)
        ya = _attention(qt, k, vt, batch, seq)
        xf = _outffn(xf, ya, ybc, lw)
    return xf.reshape(batch, seq, d)
```

```python
import functools
import math

import jax
import jax.numpy as jnp
from jax import lax
from jax.experimental import pallas as pl
from jax.experimental.pallas import tpu as pltpu

D_MODEL = 1024
HEAD_DIM = 64
N_HEADS = 8
NOPE_DIM = 64
ROPE_DIM = 32
HALF_ROPE = ROPE_DIM // 2
V_DIM = 64
Q_RANK = 384
KV_RANK = 256
ROPE_THETA = 10000.0
SG_GROUPS = 4
SG_WIDTH = 256
CHUNK = 128
CV_WIDTH = 256
MLA_WIDTH = N_HEADS * V_DIM
D_FF = 2816
EPS = 1e-6

OFF_CQ = 0
OFF_CKV = OFF_CQ + Q_RANK
OFF_KR = OFF_CKV + KV_RANK
OFF_SG = OFF_KR + ROPE_DIM
OFF_CV = OFF_SG + 2 * SG_WIDTH
IN_WIDTH = OFF_CV + 3 * CV_WIDTH

P_CQ = 0
P_SG = 384
P_CKV = 896
P_KR = 1152
P_CV = 1280
P_WIDTH = 2048

HEAD_PAD = 128
QK_WIDTH = N_HEADS * HEAD_PAD
BF16_ROWS = 16
V_SLOT = V_DIM + BF16_ROWS
VT_ROWS = N_HEADS * V_SLOT

VMEM_LIMIT_BYTES = 56 * 1024 * 1024
LANES = 128

TM_IN = 512
T_ATT = 256
TM_OUT = 512
FF_CHUNKS = ((0, 1024), (1024, 1024), (2048, 768))

_NT = (((1,), (1,)), ((), ()))
SOFTMAX_SCALE = (NOPE_DIM + ROPE_DIM) ** -0.5
LOG2E = 1.4426950408889634
MASK_VALUE = -1e30
PV_LAG = 2


def _dot(a, b):
    return jnp.dot(a, b, preferred_element_type=jnp.float32)


def _dot_nt(a, b):
    return lax.dot_general(a, b, _NT, preferred_element_type=jnp.float32)


def _rms(x, g):
    return x * lax.rsqrt(jnp.mean(x * x, axis=-1, keepdims=True) + EPS) * g


def _const_spec(shape):
    nd = len(shape)
    return pl.BlockSpec(shape, lambda *_: (0,) * nd, pipeline_mode=pl.Buffered(1))


def _rope_kernel(pos_ref, inv_ref, cos_ref, sin_ref):
    ang = pos_ref[...].astype(jnp.float32) * inv_ref[...]
    cos_ref[...] = jnp.cos(ang)
    sin_ref[...] = jnp.sin(ang)


def _rope_tables(positions):
    t = positions.size
    tn = 4096
    pos = positions.reshape(1, t)
    inv_freq = 1.0 / (ROPE_THETA ** (jnp.arange(0, HALF_ROPE, dtype=jnp.float32) / HALF_ROPE))
    return pl.pallas_call(
        _rope_kernel,
        grid=(t // tn,),
        in_specs=[pl.BlockSpec((1, tn), lambda i: (0, i)),
                  pl.BlockSpec((HALF_ROPE, 1), lambda i: (0, 0))],
        out_specs=[pl.BlockSpec((HALF_ROPE, tn), lambda i: (0, i))] * 2,
        out_shape=[jax.ShapeDtypeStruct((HALF_ROPE, t), jnp.float32)] * 2,
        name="rope_tables",
    )(pos, inv_freq.reshape(HALF_ROPE, 1))


def _inproj_kernel(steps_per_seq,
                   x_ref, g_ref, win_ref, qg_ref, wuqt_ref, kvg_ref, wk_ref, wvt_ref,
                   ropec_ref, ropes_ref, cost_ref, sint_ref, lng_ref, lnb_ref, avg_ref, wsp_ref,
                   bsp_ref, convw_ref, ong_ref, wout_f32_ref, wgate_f32_ref, wup_f32_ref,
                   wdown_f32_ref,
                   qt_ref, k_ref, vt_ref, ybc_ref, wout_ref, wgate_ref, wup_ref, wdown_ref,
                   carry_ref):
    tm = x_ref.shape[0]

    @pl.when(pl.program_id(0) % steps_per_seq == 0)
    def _():
        carry_ref[...] = jnp.zeros_like(carry_ref)

    for src_ref, dst_ref in ((wout_f32_ref, wout_ref), (wgate_f32_ref, wgate_ref),
                             (wup_f32_ref, wup_ref), (wdown_f32_ref, wdown_ref)):
        dst_ref[...] = src_ref[...].astype(jnp.bfloat16)

    h = _rms(x_ref[...], g_ref[...]).astype(jnp.bfloat16)
    z = _dot(h, win_ref[...])

    c_q = _rms(z[:, P_CQ:P_CQ + Q_RANK], qg_ref[...]).astype(jnp.bfloat16)
    qt = _dot_nt(wuqt_ref[...], c_q)
    cos_t = cost_ref[...]
    sin_t = sint_ref[...]
    for hd in range(N_HEADS):
        base = hd * HEAD_PAD
        t1 = qt[base + NOPE_DIM:base + NOPE_DIM + HALF_ROPE]
        t2 = qt[base + NOPE_DIM + HALF_ROPE:base + NOPE_DIM + ROPE_DIM]
        qt_ref[base:base + NOPE_DIM, :] = qt[base:base + NOPE_DIM].astype(jnp.bfloat16)
        qt_ref[base + NOPE_DIM:base + NOPE_DIM + HALF_ROPE, :] = (
            t1 * cos_t - t2 * sin_t).astype(jnp.bfloat16)
        qt_ref[base + NOPE_DIM + HALF_ROPE:base + NOPE_DIM + ROPE_DIM, :] = (
            t2 * cos_t + t1 * sin_t).astype(jnp.bfloat16)
        qt_ref[base + NOPE_DIM + ROPE_DIM:base + HEAD_PAD, :] = jnp.zeros(
            (HEAD_PAD - NOPE_DIM - ROPE_DIM, tm), jnp.bfloat16)

    uv = z[:, P_SG:P_SG + 2 * SG_WIDTH]
    uv = uv * (0.5 * (1.0 + jnp.tanh(math.sqrt(2.0 / math.pi) * (uv + 0.044715 * (uv * uv * uv)))))
    u = uv[:, :SG_WIDTH]
    v = uv[:, SG_WIDTH:]
    avg = avg_ref[...]

    def group_mean(a):
        hi = a.astype(jnp.bfloat16)
        lo = (a - hi.astype(jnp.float32)).astype(jnp.bfloat16)
        return _dot(hi, avg) + _dot(lo, avg)

    dv = v - group_mean(v)

    c_kv = _rms(z[:, P_CKV:P_CKV + KV_RANK], kvg_ref[...]).astype(jnp.bfloat16)
    zkr = z[:, P_KR:P_KR + LANES]
    k_rope = zkr * ropec_ref[...] + pltpu.roll(zkr, HALF_ROPE, 1) * ropes_ref[...]
    k_nope = _dot(c_kv, wk_ref[...])
    for hd in range(N_HEADS):
        slot = slice(hd * HEAD_PAD, (hd + 1) * HEAD_PAD)
        k_ref[:, slot] = (k_nope[:, slot] + k_rope).astype(jnp.bfloat16)

    vn = dv * lax.rsqrt(group_mean(dv * dv) + EPS) * lng_ref[...] + lnb_ref[...]
    vn = vn.astype(jnp.bfloat16)

    gate_b = z[:, P_CV:P_CV + CV_WIDTH]
    y = z[:, P_CV + CV_WIDTH:P_CV + 2 * CV_WIDTH] * z[:, P_CV + 2 * CV_WIDTH:P_CV + 3 * CV_WIDTH]
    prev = carry_ref[...]
    carry_ref[...] = y[tm - 8:]
    row8 = lax.broadcasted_iota(jnp.int32, (8, CV_WIDTH), 0)

    def shifted(k):
        r = pltpu.roll(y, k, 0)
        top = jnp.where(row8 < k, pltpu.roll(prev, k, 0), r[:8])
        return jnp.concatenate([top, r[8:]], axis=0)

    cw = convw_ref[...]
    conv = shifted(2) * cw[0:1] + shifted(1) * cw[1:2] + y * cw[2:3]
    ong = ong_ref[...]
    ybc_ref[:, SG_WIDTH:] = _rms(gate_b * conv, ong[:, SG_WIDTH:]).astype(jnp.bfloat16)

    rows = lax.broadcasted_iota(jnp.int32, (SG_GROUPS * CHUNK, CHUNK), 0)
    cols = lax.broadcasted_iota(jnp.int32, (SG_GROUPS * CHUNK, CHUNK), 1)
    w_causal = jnp.where((rows % CHUNK) >= cols, wsp_ref[...], 0.0).astype(jnp.bfloat16)
    lane_group = lax.broadcasted_iota(jnp.int32, (CHUNK, SG_WIDTH), 1) // HEAD_DIM
    bsp = bsp_ref[...]
    mixed = []
    for c in range(tm // CHUNK):
        o = _dot(w_causal, vn[c * CHUNK:(c + 1) * CHUNK])
        m = o[3 * CHUNK:]
        for g in (2, 1, 0):
            m = jnp.where(lane_group == g, o[g * CHUNK:(g + 1) * CHUNK], m)
        mixed.append(m + bsp)

    vt = _dot_nt(wvt_ref[...], c_kv).astype(jnp.bfloat16)
    tk = vt_ref.shape[2]
    ones = jnp.ones((BF16_ROWS, tk), jnp.bfloat16)
    for j in range(tm // tk):
        for hd in range(N_HEADS):
            vt_ref[j, hd * V_SLOT:hd * V_SLOT + V_DIM, :] = vt[hd * V_DIM:(hd + 1) * V_DIM,
                                                               j * tk:(j + 1) * tk]
            vt_ref[j, hd * V_SLOT + V_DIM:(hd + 1) * V_SLOT, :] = ones

    y_b = u * jnp.concatenate(mixed, axis=0)
    ybc_ref[:, :SG_WIDTH] = _rms(y_b, ong[:, :SG_WIDTH]).astype(jnp.bfloat16)


def _inproj(x, lw, tabs, seq):
    t = x.shape[0]
    tm = TM_IN
    rope_c, rope_s, cos_t, sin_t = tabs
    row = lambda i: (i, 0)
    col = lambda i: (0, i)
    consts = [lw["g_pre"], lw["w_in"], lw["q_norm_g"], lw["w_uq_t"], lw["kv_norm_g"], lw["w_k"],
              lw["w_v_t"]]
    consts2 = [lw["sg_ln_g"], lw["sg_ln_b"], lw["avg"], lw["w_sp"], lw["b_sp"], lw["conv_w"],
               lw["out_norm_g_bc"]]
    in_specs = ([pl.BlockSpec((tm, D_MODEL), row)]
                + [_const_spec(c.shape) for c in consts]
                + [pl.BlockSpec((tm, LANES), row),
                   pl.BlockSpec((tm, LANES), row),
                   pl.BlockSpec((HALF_ROPE, tm), col),
                   pl.BlockSpec((HALF_ROPE, tm), col)]
                + [_const_spec(c.shape) for c in consts2])
    steps = t // tm
    casts = [lw["w_out"], lw["w_gate"], lw["w_up"], lw["w_down"]]
    cast_specs = []
    for w in casts:
        rows = -(-w.shape[0] // (steps * BF16_ROWS)) * BF16_ROWS
        while w.shape[0] % rows:
            rows += BF16_ROWS
        last = w.shape[0] // rows - 1
        cast_specs.append(pl.BlockSpec(
            (rows, w.shape[1]), lambda i, last=last: (jnp.minimum(i, last), 0)))
    in_specs = in_specs + cast_specs
    out_shape = [jax.ShapeDtypeStruct((QK_WIDTH, t), jnp.bfloat16),
                 jax.ShapeDtypeStruct((t, QK_WIDTH), jnp.bfloat16),
                 jax.ShapeDtypeStruct((t // T_ATT, VT_ROWS, T_ATT), jnp.bfloat16),
                 jax.ShapeDtypeStruct((t, SG_WIDTH + CV_WIDTH), jnp.bfloat16)]
    out_shape += [jax.ShapeDtypeStruct(w.shape, jnp.bfloat16) for w in casts]
    out_specs = [pl.BlockSpec((QK_WIDTH, tm), col),
                 pl.BlockSpec((tm, QK_WIDTH), row),
                 pl.BlockSpec((tm // T_ATT, VT_ROWS, T_ATT), lambda i: (i, 0, 0)),
                 pl.BlockSpec((tm, SG_WIDTH + CV_WIDTH), row)]
    out_specs += cast_specs
    return pl.pallas_call(
        functools.partial(_inproj_kernel, seq // tm),
        grid=(t // tm,),
        in_specs=in_specs,
        out_specs=out_specs,
        out_shape=out_shape,
        scratch_shapes=[pltpu.VMEM((8, CV_WIDTH), jnp.float32)],
        compiler_params=pltpu.CompilerParams(
            dimension_semantics=("arbitrary",), vmem_limit_bytes=VMEM_LIMIT_BYTES),
        name="inproj",
    )(x, *consts, rope_c, rope_s, cos_t, sin_t, *consts2, *casts)


def _attn_kernel(qt_ref, k_ref, vt_ref, o_ref, m_ref, acc_ref, sa_ref, sb_ref):
    tq = qt_ref.shape[1]
    i = pl.program_id(1)
    m_ref[...] = jnp.full(m_ref.shape, MASK_VALUE, jnp.float32)
    acc_ref[...] = jnp.zeros(acc_ref.shape, jnp.float32)

    def stage(kb_next, dst_ref, kb_cur, src_ref, masked):
        probs, m_all = [], []
        if kb_cur is not None:
            if masked:
                key_row = lax.broadcasted_iota(jnp.int32, (tq, tq), 0)
                qry_col = lax.broadcasted_iota(jnp.int32, (tq, tq), 1)
                causal = key_row <= qry_col
            for hd in range(N_HEADS):
                s = src_ref[hd]
                if masked:
                    s = jnp.where(causal, s, MASK_VALUE)
                m = m_ref[hd:hd + 1, :]
                m_new = jnp.maximum(m, jnp.max(s, axis=0, keepdims=True))
                probs.append((jnp.exp2(m - m_new), jnp.exp2(s - m_new).astype(jnp.bfloat16)))
                m_all.append(m_new)

        def score_dot(hd):
            start = pl.multiple_of(kb_next * tq, tq)
            q_h = qt_ref[hd * HEAD_PAD:(hd + 1) * HEAD_PAD, :]
            k_blk = k_ref[pl.ds(start, tq), hd * HEAD_PAD:(hd + 1) * HEAD_PAD]
            dst_ref[hd] = _dot(k_blk, q_h)

        def pv_dot(hd):
            alpha, p = probs[hd]
            slab = slice(hd * V_SLOT, (hd + 1) * V_SLOT)
            v_blk = vt_ref[kb_cur, slab, :]
            acc_ref[slab, :] = alpha * acc_ref[slab, :] + _dot(v_blk, p)

        for j in range(N_HEADS + PV_LAG):
            if kb_next is not None and j < N_HEADS:
                score_dot(j)
            if kb_cur is not None and j >= PV_LAG:
                pv_dot(j - PV_LAG)
        for hd in range(len(m_all)):
            m_ref[hd:hd + 1, :] = m_all[hd]

    stage(0, sa_ref, None, None, False)

    def pair(t, carry):
        kb = 2 * t
        stage(kb + 1, sb_ref, kb, sa_ref, False)
        stage(kb + 2, sa_ref, kb + 1, sb_ref, False)
        return carry

    def quad(t, carry):
        pair(2 * t, carry)
        return pair(2 * t + 1, carry)

    lax.fori_loop(0, i // 4, quad, 0)
    lax.fori_loop(2 * (i // 4), i // 2, pair, 0)

    @pl.when(i % 2 == 0)
    def _():
        stage(None, None, i, sa_ref, True)

    @pl.when(i % 2 == 1)
    def _():
        stage(i, sb_ref, i - 1, sa_ref, False)
        stage(None, None, i, sb_ref, True)

    outs = []
    for hd in range(N_HEADS):
        base = hd * V_SLOT
        outs.append(acc_ref[base:base + V_DIM, :] / acc_ref[base + V_DIM:base + V_DIM + 1, :])
    o_ref[...] = jnp.concatenate(outs, axis=0).T.astype(o_ref.dtype)


def _attention(qt, k, vt, batch, seq):
    t = k.shape[0]
    tq = T_ATT
    nq = seq // tq
    return pl.pallas_call(
        _attn_kernel,
        grid=(batch, nq),
        in_specs=[pl.BlockSpec((QK_WIDTH, tq), lambda b, i: (0, b * nq + i)),
                  pl.BlockSpec((seq, QK_WIDTH), lambda b, i: (b, 0)),
                  pl.BlockSpec((nq, VT_ROWS, tq), lambda b, i: (b, 0, 0))],
        out_specs=pl.BlockSpec((tq, MLA_WIDTH), lambda b, i: (b * nq + i, 0)),
        out_shape=jax.ShapeDtypeStruct((t, MLA_WIDTH), jnp.bfloat16),
        scratch_shapes=[pltpu.VMEM((N_HEADS, tq), jnp.float32),
                        pltpu.VMEM((VT_ROWS, tq), jnp.float32),
                        pltpu.VMEM((N_HEADS, tq, tq), jnp.float32),
                        pltpu.VMEM((N_HEADS, tq, tq), jnp.float32)],
        compiler_params=pltpu.CompilerParams(
            dimension_semantics=("arbitrary", "arbitrary"), vmem_limit_bytes=VMEM_LIMIT_BYTES),
        name="attention",
    )(qt, k, vt)


def _outffn_kernel(x_ref, ya_ref, ybc_ref, onga_ref, wout_ref, gpost_ref, gffn_ref,
                   wgate_ref, wup_ref, wdown_ref, gffn_post_ref, o_ref):
    tm = x_ref.shape[0]
    halves = (slice(0, tm // 2), slice(tm // 2, tm))
    x1, h = [], []
    for rows in halves:
        ya = _rms(ya_ref[rows, :].astype(jnp.float32), onga_ref[...]).astype(jnp.bfloat16)
        o = _dot(ya, wout_ref[:MLA_WIDTH, :]) + _dot(ybc_ref[rows, :], wout_ref[MLA_WIDTH:, :])
        x1.append(x_ref[rows, :] + _rms(o, gpost_ref[...]))
    for k in range(2):
        h.append(_rms(x1[k], gffn_ref[...]).astype(jnp.bfloat16))
    d = [jnp.zeros(x1[0].shape, jnp.float32)] * 2
    for start, size in FF_CHUNKS:
        gate_up = [(_dot(h[k], wgate_ref[:, start:start + size]),
                    _dot(h[k], wup_ref[:, start:start + size])) for k in range(2)]
        for k in range(2):
            gate, up = gate_up[k]
            f = (gate * (1.0 / (1.0 + jnp.exp(-gate))) * up).astype(jnp.bfloat16)
            d[k] = d[k] + _dot(f, wdown_ref[start:start + size, :])
    for k, rows in enumerate(halves):
        o_ref[rows, :] = x1[k] + _rms(d[k], gffn_post_ref[...])


def _outffn(x, ya, ybc, lw, w_bf16):
    t = x.shape[0]
    tm = TM_OUT
    row = lambda i: (i, 0)
    w_out, w_gate, w_up, w_down = w_bf16
    consts = [lw["out_norm_g_a"], w_out, lw["g_post"], lw["g_ffn_pre"], w_gate,
              w_up, w_down, lw["g_ffn_post"]]
    return pl.pallas_call(
        _outffn_kernel,
        grid=(t // tm,),
        in_specs=[pl.BlockSpec((tm, D_MODEL), row),
                  pl.BlockSpec((tm, MLA_WIDTH), row),
                  pl.BlockSpec((tm, SG_WIDTH + CV_WIDTH), row)]
                 + [_const_spec(c.shape) for c in consts],
        out_specs=pl.BlockSpec((tm, D_MODEL), row),
        out_shape=jax.ShapeDtypeStruct((t, D_MODEL), jnp.float32),
        compiler_params=pltpu.CompilerParams(
            dimension_semantics=("arbitrary",), vmem_limit_bytes=VMEM_LIMIT_BYTES),
        name="outffn",
    )(x, ya, ybc, *consts)


def _prep_layer(l, p):
    bf = jnp.bfloat16
    f32 = jnp.float32
    w_in = p["w_in"][l]
    d = w_in.shape[0]
    t1 = w_in[:, OFF_KR:OFF_KR + HALF_ROPE]
    t2 = w_in[:, OFF_KR + HALF_ROPE:OFF_SG]
    w_in_p = jnp.concatenate([
        w_in[:, OFF_CQ:OFF_CKV],
        w_in[:, OFF_SG:OFF_CV],
        w_in[:, OFF_CKV:OFF_KR],
        jnp.zeros((d, NOPE_DIM - HALF_ROPE), f32), t2, t1, t2,
        jnp.zeros((d, HEAD_PAD - NOPE_DIM - ROPE_DIM), f32),
        w_in[:, OFF_CV:IN_WIDTH]], axis=1).astype(bf)

    w_uq = p["w_uq"][l].reshape(Q_RANK, N_HEADS, NOPE_DIM + ROPE_DIM) * (SOFTMAX_SCALE * LOG2E)
    w_uq = jnp.pad(w_uq, ((0, 0), (0, 0), (0, HEAD_PAD - NOPE_DIM - ROPE_DIM)))
    w_uq_t = w_uq.reshape(Q_RANK, QK_WIDTH).T.astype(bf)

    w_ukv = p["w_ukv"][l].reshape(KV_RANK, N_HEADS, NOPE_DIM + V_DIM)
    w_k = jnp.pad(w_ukv[:, :, :NOPE_DIM], ((0, 0), (0, 0), (0, HEAD_PAD - NOPE_DIM)))
    w_k = w_k.reshape(KV_RANK, QK_WIDTH).astype(bf)
    w_v_t = w_ukv[:, :, NOPE_DIM:].reshape(KV_RANK, MLA_WIDTH).T.astype(bf)

    g = p["out_norm_g"][l]
    return {
        "g_pre": p["mix_pre_g"][l].reshape(1, -1),
        "w_in": w_in_p,
        "q_norm_g": p["q_norm_g"][l].reshape(1, -1),
        "w_uq_t": w_uq_t,
        "kv_norm_g": p["kv_norm_g"][l].reshape(1, -1),
        "w_k": w_k,
        "w_v_t": w_v_t,
        "sg_ln_g": p["sg_ln_g"][l].reshape(1, -1),
        "sg_ln_b": p["sg_ln_b"][l].reshape(1, -1),
        "w_sp": p["w_sp"][l].reshape(SG_GROUPS * CHUNK, CHUNK),
        "b_sp": jnp.repeat(p["b_sp"][l].T, HEAD_DIM, axis=1),
        "conv_w": p["conv_w"][l],
        "out_norm_g_a": g[:MLA_WIDTH].reshape(1, -1),
        "out_norm_g_bc": g[MLA_WIDTH:].reshape(1, -1),
        "w_out": p["w_out"][l],
        "g_post": p["mix_post_g"][l].reshape(1, -1),
        "g_ffn_pre": p["ffn_pre_g"][l].reshape(1, -1),
        "w_gate": p["w_gate"][l],
        "w_up": p["w_up"][l],
        "w_down": p["w_down"][l],
        "g_ffn_post": p["ffn_post_g"][l].reshape(1, -1),
    }


def _group_average_matrix():
    grp = jnp.arange(SG_WIDTH) // HEAD_DIM
    return ((grp[:, None] == grp[None, :]).astype(jnp.float32) / HEAD_DIM).astype(jnp.bfloat16)


def _key_rope_tables(cos_r, sin_r):
    t = cos_r.shape[0]
    left = jnp.zeros((t, NOPE_DIM), jnp.float32)
    right = jnp.zeros((t, HEAD_PAD - NOPE_DIM - ROPE_DIM), jnp.float32)
    rope_c = jnp.concatenate([left, cos_r, cos_r, right], axis=1)
    rope_s = jnp.concatenate([left, -sin_r, sin_r, right], axis=1)
    return rope_c, rope_s


def kernel(x, positions, mix_pre_g, mix_post_g, ffn_pre_g, ffn_post_g, w_in, q_norm_g, w_uq,
           kv_norm_g, w_ukv, sg_ln_g, sg_ln_b, w_sp, b_sp, conv_w, out_norm_g, w_out,
           w_gate, w_up, w_down):
    params = dict(mix_pre_g=mix_pre_g, mix_post_g=mix_post_g, ffn_pre_g=ffn_pre_g,
                  ffn_post_g=ffn_post_g, w_in=w_in, q_norm_g=q_norm_g, w_uq=w_uq,
                  kv_norm_g=kv_norm_g, w_ukv=w_ukv, sg_ln_g=sg_ln_g, sg_ln_b=sg_ln_b,
                  w_sp=w_sp, b_sp=b_sp, conv_w=conv_w, out_norm_g=out_norm_g, w_out=w_out,
                  w_gate=w_gate, w_up=w_up, w_down=w_down)
    batch, seq, d = x.shape
    depth = w_in.shape[0]
    assert seq % TM_IN == 0 and seq % T_ATT == 0 and TM_IN % T_ATT == 0
    assert (batch * seq) % TM_OUT == 0 and d == D_MODEL

    cos_t, sin_t = _rope_tables(positions)
    rope_c, rope_s = _key_rope_tables(cos_t.T, sin_t.T)
    avg = _group_average_matrix()

    xf = x.reshape(batch * seq, d)
    for l in range(depth):
        lw = _prep_layer(l, params)
        lw["avg"] = avg
        qt, k, vt, ybc, *w_bf16 = _inproj(xf, lw, (rope_c, rope_s, cos_t, sin_t), seq)
        ya = _attention(qt, k, vt, batch, seq)
        xf = _outffn(xf, ya, ybc, lw, w_bf16)
    return xf.reshape(batch, seq, d)
```

```python
import functools
import math

import jax
import jax.numpy as jnp
from jax import lax
from jax.experimental import pallas as pl
from jax.experimental.pallas import tpu as pltpu

D_MODEL = 1024
HEAD_DIM = 64
N_HEADS = 8
NOPE_DIM = 64
ROPE_DIM = 32
HALF_ROPE = ROPE_DIM // 2
V_DIM = 64
Q_RANK = 384
KV_RANK = 256
ROPE_THETA = 10000.0
SG_GROUPS = 4
SG_WIDTH = 256
CHUNK = 128
CV_WIDTH = 256
MLA_WIDTH = N_HEADS * V_DIM
D_FF = 2816
EPS = 1e-6

OFF_CQ = 0
OFF_CKV = OFF_CQ + Q_RANK
OFF_KR = OFF_CKV + KV_RANK
OFF_SG = OFF_KR + ROPE_DIM
OFF_CV = OFF_SG + 2 * SG_WIDTH
IN_WIDTH = OFF_CV + 3 * CV_WIDTH

P_CQ = 0
P_SG = 384
P_CKV = 896
P_KR = 1152
P_CV = 1280
P_WIDTH = 2048

HEAD_PAD = 128
QK_WIDTH = N_HEADS * HEAD_PAD
BF16_ROWS = 16
V_SLOT = V_DIM + BF16_ROWS
VT_ROWS = N_HEADS * V_SLOT

VMEM_LIMIT_BYTES = 56 * 1024 * 1024
LANES = 128

TM_IN = 512
T_ATT = 256
TM_OUT = 512
FF_CHUNKS = ((0, 1024), (1024, 1024), (2048, 768))

_NT = (((1,), (1,)), ((), ()))
SOFTMAX_SCALE = (NOPE_DIM + ROPE_DIM) ** -0.5
LOG2E = 1.4426950408889634
MASK_VALUE = -1e30
PV_LAG = 2


def _dot(a, b):
    return jnp.dot(a, b, preferred_element_type=jnp.float32)


def _dot_nt(a, b):
    return lax.dot_general(a, b, _NT, preferred_element_type=jnp.float32)


def _rms(x, g):
    return x * lax.rsqrt(jnp.mean(x * x, axis=-1, keepdims=True) + EPS) * g


def _const_spec(shape):
    nd = len(shape)
    return pl.BlockSpec(shape, lambda *_: (0,) * nd, pipeline_mode=pl.Buffered(1))


def _rope_kernel(pos_ref, inv_ref, cos_ref, sin_ref):
    ang = pos_ref[...].astype(jnp.float32) * inv_ref[...]
    cos_ref[...] = jnp.cos(ang)
    sin_ref[...] = jnp.sin(ang)


def _rope_tables(positions):
    t = positions.size
    tn = 4096
    pos = positions.reshape(1, t)
    inv_freq = 1.0 / (ROPE_THETA ** (jnp.arange(0, HALF_ROPE, dtype=jnp.float32) / HALF_ROPE))
    return pl.pallas_call(
        _rope_kernel,
        grid=(t // tn,),
        in_specs=[pl.BlockSpec((1, tn), lambda i: (0, i)),
                  pl.BlockSpec((HALF_ROPE, 1), lambda i: (0, 0))],
        out_specs=[pl.BlockSpec((HALF_ROPE, tn), lambda i: (0, i))] * 2,
        out_shape=[jax.ShapeDtypeStruct((HALF_ROPE, t), jnp.float32)] * 2,
        name="rope_tables",
    )(pos, inv_freq.reshape(HALF_ROPE, 1))


def _inproj_kernel(steps_per_seq,
                   x_ref, g_ref, win_ref, qg_ref, wuqt_ref, kvg_ref, wk_ref, wvt_ref,
                   ropec_ref, ropes_ref, cost_ref, sint_ref, lng_ref, lnb_ref, avg_ref, wsp_ref,
                   bsp_ref, convw_ref, ong_ref, wout_f32_ref, wgate_f32_ref, wup_f32_ref,
                   wdown_f32_ref,
                   qt_ref, k_ref, vt_ref, ybc_ref, wout_ref, wgate_ref, wup_ref, wdown_ref,
                   carry_ref):
    tm = x_ref.shape[0]
    th = tm // 2
    halves = (slice(0, th), slice(th, tm))

    @pl.when(pl.program_id(0) % steps_per_seq == 0)
    def _():
        carry_ref[...] = jnp.zeros_like(carry_ref)

    for src_ref, dst_ref in ((wout_f32_ref, wout_ref), (wgate_f32_ref, wgate_ref),
                             (wup_f32_ref, wup_ref), (wdown_f32_ref, wdown_ref)):
        dst_ref[...] = src_ref[...].astype(jnp.bfloat16)

    avg = avg_ref[...]
    ong = ong_ref[...]
    cw = convw_ref[...]
    bsp = bsp_ref[...]
    tk = vt_ref.shape[2]

    def group_mean(a):
        hi = a.astype(jnp.bfloat16)
        lo = (a - hi.astype(jnp.float32)).astype(jnp.bfloat16)
        return _dot(hi, avg) + _dot(lo, avg)

    z = [_dot(_rms(x_ref[rows, :], g_ref[...]).astype(jnp.bfloat16), win_ref[...])
         for rows in halves]

    for zh, rows in zip(z, halves):
        c_q = _rms(zh[:, P_CQ:P_CQ + Q_RANK], qg_ref[...]).astype(jnp.bfloat16)
        qt = _dot_nt(wuqt_ref[...], c_q)
        cos_t = cost_ref[:, rows]
        sin_t = sint_ref[:, rows]
        for hd in range(N_HEADS):
            base = hd * HEAD_PAD
            t1 = qt[base + NOPE_DIM:base + NOPE_DIM + HALF_ROPE]
            t2 = qt[base + NOPE_DIM + HALF_ROPE:base + NOPE_DIM + ROPE_DIM]
            qt_ref[base:base + NOPE_DIM, rows] = qt[base:base + NOPE_DIM].astype(jnp.bfloat16)
            qt_ref[base + NOPE_DIM:base + NOPE_DIM + HALF_ROPE, rows] = (
                t1 * cos_t - t2 * sin_t).astype(jnp.bfloat16)
            qt_ref[base + NOPE_DIM + HALF_ROPE:base + NOPE_DIM + ROPE_DIM, rows] = (
                t2 * cos_t + t1 * sin_t).astype(jnp.bfloat16)
            qt_ref[base + NOPE_DIM + ROPE_DIM:base + HEAD_PAD, rows] = jnp.zeros(
                (HEAD_PAD - NOPE_DIM - ROPE_DIM, th), jnp.bfloat16)

    u, dv = [], []
    for zh in z:
        uv = zh[:, P_SG:P_SG + 2 * SG_WIDTH]
        uv = uv * (0.5 * (1.0 + jnp.tanh(
            math.sqrt(2.0 / math.pi) * (uv + 0.044715 * (uv * uv * uv)))))
        u.append(uv[:, :SG_WIDTH])
        v = uv[:, SG_WIDTH:]
        dv.append(v - group_mean(v))

    c_kv = []
    for zh, rows in zip(z, halves):
        c_kv.append(_rms(zh[:, P_CKV:P_CKV + KV_RANK], kvg_ref[...]).astype(jnp.bfloat16))
        zkr = zh[:, P_KR:P_KR + LANES]
        k_rope = (zkr * ropec_ref[rows, :]
                  + pltpu.roll(zkr, HALF_ROPE, 1) * ropes_ref[rows, :])
        k_nope = _dot(c_kv[-1], wk_ref[...])
        for hd in range(N_HEADS):
            slot = slice(hd * HEAD_PAD, (hd + 1) * HEAD_PAD)
            k_ref[rows, slot] = (k_nope[:, slot] + k_rope).astype(jnp.bfloat16)

    vn = [(d * lax.rsqrt(group_mean(d * d) + EPS) * lng_ref[...] + lnb_ref[...]
           ).astype(jnp.bfloat16) for d in dv]

    prev = carry_ref[...]
    row8 = lax.broadcasted_iota(jnp.int32, (8, CV_WIDTH), 0)
    for zh, rows in zip(z, halves):
        gate_b = zh[:, P_CV:P_CV + CV_WIDTH]
        y = (zh[:, P_CV + CV_WIDTH:P_CV + 2 * CV_WIDTH]
             * zh[:, P_CV + 2 * CV_WIDTH:P_CV + 3 * CV_WIDTH])

        def shifted(k, y=y, prev=prev):
            r = pltpu.roll(y, k, 0)
            top = jnp.where(row8 < k, pltpu.roll(prev, k, 0), r[:8])
            return jnp.concatenate([top, r[8:]], axis=0)

        conv = shifted(2) * cw[0:1] + shifted(1) * cw[1:2] + y * cw[2:3]
        ybc_ref[rows, SG_WIDTH:] = _rms(gate_b * conv, ong[:, SG_WIDTH:]).astype(jnp.bfloat16)
        prev = y[th - 8:]
    carry_ref[...] = prev

    rows_i = lax.broadcasted_iota(jnp.int32, (SG_GROUPS * CHUNK, CHUNK), 0)
    cols_i = lax.broadcasted_iota(jnp.int32, (SG_GROUPS * CHUNK, CHUNK), 1)
    w_causal = jnp.where((rows_i % CHUNK) >= cols_i, wsp_ref[...], 0.0).astype(jnp.bfloat16)
    lane_group = lax.broadcasted_iota(jnp.int32, (CHUNK, SG_WIDTH), 1) // HEAD_DIM
    mixed = []
    for vh in vn:
        chunks = []
        for c in range(th // CHUNK):
            o = _dot(w_causal, vh[c * CHUNK:(c + 1) * CHUNK])
            m = o[3 * CHUNK:]
            for g in (2, 1, 0):
                m = jnp.where(lane_group == g, o[g * CHUNK:(g + 1) * CHUNK], m)
            chunks.append(m + bsp)
        mixed.append(jnp.concatenate(chunks, axis=0))

    ones = jnp.ones((BF16_ROWS, tk), jnp.bfloat16)
    for hf, ckv in enumerate(c_kv):
        vt = _dot_nt(wvt_ref[...], ckv).astype(jnp.bfloat16)
        for j in range(th // tk):
            blk = hf * (th // tk) + j
            for hd in range(N_HEADS):
                vt_ref[blk, hd * V_SLOT:hd * V_SLOT + V_DIM, :] = vt[hd * V_DIM:(hd + 1) * V_DIM,
                                                                     j * tk:(j + 1) * tk]
                vt_ref[blk, hd * V_SLOT + V_DIM:(hd + 1) * V_SLOT, :] = ones

    for uh, mh, rows in zip(u, mixed, halves):
        ybc_ref[rows, :SG_WIDTH] = _rms(uh * mh, ong[:, :SG_WIDTH]).astype(jnp.bfloat16)


def _inproj(x, lw, tabs, seq):
    t = x.shape[0]
    tm = TM_IN
    rope_c, rope_s, cos_t, sin_t = tabs
    row = lambda i: (i, 0)
    col = lambda i: (0, i)
    consts = [lw["g_pre"], lw["w_in"], lw["q_norm_g"], lw["w_uq_t"], lw["kv_norm_g"], lw["w_k"],
              lw["w_v_t"]]
    consts2 = [lw["sg_ln_g"], lw["sg_ln_b"], lw["avg"], lw["w_sp"], lw["b_sp"], lw["conv_w"],
               lw["out_norm_g_bc"]]
    in_specs = ([pl.BlockSpec((tm, D_MODEL), row)]
                + [_const_spec(c.shape) for c in consts]
                + [pl.BlockSpec((tm, LANES), row),
                   pl.BlockSpec((tm, LANES), row),
                   pl.BlockSpec((HALF_ROPE, tm), col),
                   pl.BlockSpec((HALF_ROPE, tm), col)]
                + [_const_spec(c.shape) for c in consts2])
    steps = t // tm
    layer, casts = lw["layer"], lw["stacked_f32"]
    cast_in_specs, cast_out_specs = [], []
    for w in casts:
        n_rows, n_cols = w.shape[1:]
        rows = -(-n_rows // (steps * BF16_ROWS)) * BF16_ROWS
        while n_rows % rows:
            rows += BF16_ROWS
        last = n_rows // rows - 1
        cast_in_specs.append(pl.BlockSpec(
            (None, rows, n_cols), lambda i, last=last: (layer, jnp.minimum(i, last), 0)))
        cast_out_specs.append(pl.BlockSpec(
            (rows, n_cols), lambda i, last=last: (jnp.minimum(i, last), 0)))
    in_specs = in_specs + cast_in_specs
    out_shape = [jax.ShapeDtypeStruct((QK_WIDTH, t), jnp.bfloat16),
                 jax.ShapeDtypeStruct((t, QK_WIDTH), jnp.bfloat16),
                 jax.ShapeDtypeStruct((t // T_ATT, VT_ROWS, T_ATT), jnp.bfloat16),
                 jax.ShapeDtypeStruct((t, SG_WIDTH + CV_WIDTH), jnp.bfloat16)]
    out_shape += [jax.ShapeDtypeStruct(w.shape[1:], jnp.bfloat16) for w in casts]
    out_specs = [pl.BlockSpec((QK_WIDTH, tm), col),
                 pl.BlockSpec((tm, QK_WIDTH), row),
                 pl.BlockSpec((tm // T_ATT, VT_ROWS, T_ATT), lambda i: (i, 0, 0)),
                 pl.BlockSpec((tm, SG_WIDTH + CV_WIDTH), row)]
    out_specs += cast_out_specs
    return pl.pallas_call(
        functools.partial(_inproj_kernel, seq // tm),
        grid=(t // tm,),
        in_specs=in_specs,
        out_specs=out_specs,
        out_shape=out_shape,
        scratch_shapes=[pltpu.VMEM((8, CV_WIDTH), jnp.float32)],
        compiler_params=pltpu.CompilerParams(
            dimension_semantics=("arbitrary",), vmem_limit_bytes=VMEM_LIMIT_BYTES),
        name="inproj",
    )(x, *consts, rope_c, rope_s, cos_t, sin_t, *consts2, *casts)


def _attn_kernel(qt_ref, k_ref, vt_ref, o_ref, m_ref, acc_ref, sa_ref, sb_ref):
    tk = sa_ref.shape[1]
    g = pl.program_id(1)
    m_ref[...] = jnp.full(m_ref.shape, MASK_VALUE, jnp.float32)
    acc_ref[...] = jnp.zeros(acc_ref.shape, jnp.float32)
    lo = slice(0, tk)
    hi = slice(tk, 2 * tk)

    def stage(kb_next, dst_ref, cols_next, kb_cur, src_ref, cols_cur, masked=False):
        probs, m_all = [], []
        if kb_cur is not None:
            if masked:
                key_row = lax.broadcasted_iota(jnp.int32, (tk, tk), 0)
                qry_col = lax.broadcasted_iota(jnp.int32, (tk, tk), 1)
                causal = key_row <= qry_col
            for hd in range(N_HEADS):
                s = src_ref[hd]
                if masked:
                    s = jnp.where(causal, s, MASK_VALUE)
                m = m_ref[hd:hd + 1, cols_cur]
                m_new = jnp.maximum(m, jnp.max(s, axis=0, keepdims=True))
                probs.append((jnp.exp2(m - m_new), jnp.exp2(s - m_new).astype(jnp.bfloat16)))
                m_all.append(m_new)

        def score_dot(hd):
            start = pl.multiple_of(kb_next * tk, tk)
            q_h = qt_ref[hd * HEAD_PAD:(hd + 1) * HEAD_PAD, cols_next]
            k_blk = k_ref[pl.ds(start, tk), hd * HEAD_PAD:(hd + 1) * HEAD_PAD]
            dst_ref[hd] = _dot(k_blk, q_h)

        def pv_dot(hd):
            alpha, p = probs[hd]
            slab = slice(hd * V_SLOT, (hd + 1) * V_SLOT)
            v_blk = vt_ref[kb_cur, slab, :]
            acc_ref[slab, cols_cur] = alpha * acc_ref[slab, cols_cur] + _dot(v_blk, p)

        for j in range(N_HEADS + PV_LAG):
            if kb_next is not None and j < N_HEADS:
                score_dot(j)
            if kb_cur is not None and j >= PV_LAG:
                pv_dot(j - PV_LAG)
        for hd in range(len(m_all)):
            m_ref[hd:hd + 1, cols_cur] = m_all[hd]

    def unmasked_blocks(cols, first_ref, second_ref):
        def pair(t, carry):
            kb = 2 * t
            stage(kb + 1, second_ref, cols, kb, first_ref, cols)
            stage(kb + 2, first_ref, cols, kb + 1, second_ref, cols)
            return carry

        def quad(t, carry):
            pair(2 * t, carry)
            return pair(2 * t + 1, carry)

        def octet(t, carry):
            quad(2 * t, carry)
            return quad(2 * t + 1, carry)

        lax.fori_loop(0, g // 4, octet, 0)
        lax.fori_loop(2 * (g // 4), g // 2, quad, 0)
        lax.fori_loop(2 * (g // 2), g, pair, 0)

    stage(0, sa_ref, lo, None, None, None)
    unmasked_blocks(lo, sa_ref, sb_ref)
    stage(0, sb_ref, hi, 2 * g, sa_ref, lo, masked=True)
    unmasked_blocks(hi, sb_ref, sa_ref)
    stage(2 * g + 1, sa_ref, hi, 2 * g, sb_ref, hi)
    stage(None, None, None, 2 * g + 1, sa_ref, hi, masked=True)

    outs = []
    for hd in range(N_HEADS):
        base = hd * V_SLOT
        outs.append(acc_ref[base:base + V_DIM, :] / acc_ref[base + V_DIM:base + V_DIM + 1, :])
    o_ref[...] = jnp.concatenate(outs, axis=0).T.astype(o_ref.dtype)


def _attention(qt, k, vt, batch, seq):
    t = k.shape[0]
    tk = T_ATT
    tq = 2 * tk
    nq = seq // tq
    nk = seq // tk
    return pl.pallas_call(
        _attn_kernel,
        grid=(batch, nq),
        in_specs=[pl.BlockSpec((QK_WIDTH, tq), lambda b, i: (0, b * nq + i)),
                  pl.BlockSpec((seq, QK_WIDTH), lambda b, i: (b, 0)),
                  pl.BlockSpec((nk, VT_ROWS, tk), lambda b, i: (b, 0, 0))],
        out_specs=pl.BlockSpec((tq, MLA_WIDTH), lambda b, i: (b * nq + i, 0)),
        out_shape=jax.ShapeDtypeStruct((t, MLA_WIDTH), jnp.bfloat16),
        scratch_shapes=[pltpu.VMEM((N_HEADS, tq), jnp.float32),
                        pltpu.VMEM((VT_ROWS, tq), jnp.float32),
                        pltpu.VMEM((N_HEADS, tk, tk), jnp.float32),
                        pltpu.VMEM((N_HEADS, tk, tk), jnp.float32)],
        compiler_params=pltpu.CompilerParams(
            dimension_semantics=("arbitrary", "arbitrary"), vmem_limit_bytes=VMEM_LIMIT_BYTES),
        name="attention",
    )(qt, k, vt)


def _outffn_kernel(x_ref, ya_ref, ybc_ref, onga_ref, wout_ref, gpost_ref, gffn_ref,
                   wgate_ref, wup_ref, wdown_ref, gffn_post_ref, o_ref):
    tm = x_ref.shape[0]
    halves = (slice(0, tm // 2), slice(tm // 2, tm))
    x1, h = [], []
    for rows in halves:
        ya = _rms(ya_ref[rows, :].astype(jnp.float32), onga_ref[...]).astype(jnp.bfloat16)
        o = _dot(ya, wout_ref[:MLA_WIDTH, :]) + _dot(ybc_ref[rows, :], wout_ref[MLA_WIDTH:, :])
        x1.append(x_ref[rows, :] + _rms(o, gpost_ref[...]))
    for k in range(2):
        h.append(_rms(x1[k], gffn_ref[...]).astype(jnp.bfloat16))
    d = [jnp.zeros(x1[0].shape, jnp.float32)] * 2
    for start, size in FF_CHUNKS:
        gate_up = [(_dot(h[k], wgate_ref[:, start:start + size]),
                    _dot(h[k], wup_ref[:, start:start + size])) for k in range(2)]
        for k in range(2):
            gate, up = gate_up[k]
            f = (gate * (1.0 / (1.0 + jnp.exp(-gate))) * up).astype(jnp.bfloat16)
            d[k] = d[k] + _dot(f, wdown_ref[start:start + size, :])
    for k, rows in enumerate(halves):
        o_ref[rows, :] = x1[k] + _rms(d[k], gffn_post_ref[...])


def _outffn(x, ya, ybc, lw, w_bf16):
    t = x.shape[0]
    tm = TM_OUT
    row = lambda i: (i, 0)
    w_out, w_gate, w_up, w_down = w_bf16
    consts = [lw["out_norm_g_a"], w_out, lw["g_post"], lw["g_ffn_pre"], w_gate,
              w_up, w_down, lw["g_ffn_post"]]
    return pl.pallas_call(
        _outffn_kernel,
        grid=(t // tm,),
        in_specs=[pl.BlockSpec((tm, D_MODEL), row),
                  pl.BlockSpec((tm, MLA_WIDTH), row),
                  pl.BlockSpec((tm, SG_WIDTH + CV_WIDTH), row)]
                 + [_const_spec(c.shape) for c in consts],
        out_specs=pl.BlockSpec((tm, D_MODEL), row),
        out_shape=jax.ShapeDtypeStruct((t, D_MODEL), jnp.float32),
        compiler_params=pltpu.CompilerParams(
            dimension_semantics=("arbitrary",), vmem_limit_bytes=VMEM_LIMIT_BYTES),
        name="outffn",
    )(x, ya, ybc, *consts)


def _prep_layer(l, p):
    bf = jnp.bfloat16
    f32 = jnp.float32
    w_in = p["w_in"][l]
    d = w_in.shape[0]
    t1 = w_in[:, OFF_KR:OFF_KR + HALF_ROPE]
    t2 = w_in[:, OFF_KR + HALF_ROPE:OFF_SG]
    w_in_p = jnp.concatenate([
        w_in[:, OFF_CQ:OFF_CKV],
        w_in[:, OFF_SG:OFF_CV],
        w_in[:, OFF_CKV:OFF_KR],
        jnp.zeros((d, NOPE_DIM - HALF_ROPE), f32), t2, t1, t2,
        jnp.zeros((d, HEAD_PAD - NOPE_DIM - ROPE_DIM), f32),
        w_in[:, OFF_CV:IN_WIDTH]], axis=1).astype(bf)

    w_uq = p["w_uq"][l].reshape(Q_RANK, N_HEADS, NOPE_DIM + ROPE_DIM) * (SOFTMAX_SCALE * LOG2E)
    w_uq = jnp.pad(w_uq, ((0, 0), (0, 0), (0, HEAD_PAD - NOPE_DIM - ROPE_DIM)))
    w_uq_t = w_uq.reshape(Q_RANK, QK_WIDTH).T.astype(bf)

    w_ukv = p["w_ukv"][l].reshape(KV_RANK, N_HEADS, NOPE_DIM + V_DIM)
    w_k = jnp.pad(w_ukv[:, :, :NOPE_DIM], ((0, 0), (0, 0), (0, HEAD_PAD - NOPE_DIM)))
    w_k = w_k.reshape(KV_RANK, QK_WIDTH).astype(bf)
    w_v_t = w_ukv[:, :, NOPE_DIM:].reshape(KV_RANK, MLA_WIDTH).T.astype(bf)

    g = p["out_norm_g"][l]
    return {
        "g_pre": p["mix_pre_g"][l].reshape(1, -1),
        "w_in": w_in_p,
        "q_norm_g": p["q_norm_g"][l].reshape(1, -1),
        "w_uq_t": w_uq_t,
        "kv_norm_g": p["kv_norm_g"][l].reshape(1, -1),
        "w_k": w_k,
        "w_v_t": w_v_t,
        "sg_ln_g": p["sg_ln_g"][l].reshape(1, -1),
        "sg_ln_b": p["sg_ln_b"][l].reshape(1, -1),
        "w_sp": p["w_sp"][l].reshape(SG_GROUPS * CHUNK, CHUNK),
        "b_sp": jnp.repeat(p["b_sp"][l].T, HEAD_DIM, axis=1),
        "conv_w": p["conv_w"][l],
        "out_norm_g_a": g[:MLA_WIDTH].reshape(1, -1),
        "out_norm_g_bc": g[MLA_WIDTH:].reshape(1, -1),
        "layer": l,
        "stacked_f32": [p["w_out"], p["w_gate"], p["w_up"], p["w_down"]],
        "g_post": p["mix_post_g"][l].reshape(1, -1),
        "g_ffn_pre": p["ffn_pre_g"][l].reshape(1, -1),
        "g_ffn_post": p["ffn_post_g"][l].reshape(1, -1),
    }


def _group_average_matrix():
    grp = jnp.arange(SG_WIDTH) // HEAD_DIM
    return ((grp[:, None] == grp[None, :]).astype(jnp.float32) / HEAD_DIM).astype(jnp.bfloat16)


def _key_rope_tables(cos_r, sin_r):
    t = cos_r.shape[0]
    left = jnp.zeros((t, NOPE_DIM), jnp.float32)
    right = jnp.zeros((t, HEAD_PAD - NOPE_DIM - ROPE_DIM), jnp.float32)
    rope_c = jnp.concatenate([left, cos_r, cos_r, right], axis=1)
    rope_s = jnp.concatenate([left, -sin_r, sin_r, right], axis=1)
    return rope_c, rope_s


def kernel(x, positions, mix_pre_g, mix_post_g, ffn_pre_g, ffn_post_g, w_in, q_norm_g, w_uq,
           kv_norm_g, w_ukv, sg_ln_g, sg_ln_b, w_sp, b_sp, conv_w, out_norm_g, w_out,
           w_gate, w_up, w_down):
    params = dict(mix_pre_g=mix_pre_g, mix_post_g=mix_post_g, ffn_pre_g=ffn_pre_g,
                  ffn_post_g=ffn_post_g, w_in=w_in, q_norm_g=q_norm_g, w_uq=w_uq,
                  kv_norm_g=kv_norm_g, w_ukv=w_ukv, sg_ln_g=sg_ln_g, sg_ln_b=sg_ln_b,
                  w_sp=w_sp, b_sp=b_sp, conv_w=conv_w, out_norm_g=out_norm_g, w_out=w_out,
                  w_gate=w_gate, w_up=w_up, w_down=w_down)
    batch, seq, d = x.shape
    depth = w_in.shape[0]
    assert seq % TM_IN == 0 and seq % (2 * T_ATT) == 0 and TM_IN % T_ATT == 0
    assert (batch * seq) % TM_OUT == 0 and d == D_MODEL

    cos_t, sin_t = _rope_tables(positions)
    rope_c, rope_s = _key_rope_tables(cos_t.T, sin_t.T)
    avg = _group_average_matrix()

    xf = x.reshape(batch * seq, d)
    for l in range(depth):
        lw = _prep_layer(l, params)
        lw["avg"] = avg
        qt, k, vt, ybc, *w_bf16 = _inproj(xf, lw, (rope_c, rope_s, cos_t, sin_t), seq)
        ya = _attention(qt, k, vt, batch, seq)
        xf = _outffn(xf, ya, ybc, lw, w_bf16)
    return xf.reshape(batch, seq, d)
```

```python
import functools
import math

import jax
import jax.numpy as jnp
from jax import lax
from jax.experimental import pallas as pl
from jax.experimental.pallas import tpu as pltpu

D_MODEL = 1024
HEAD_DIM = 64
N_HEADS = 8
NOPE_DIM = 64
ROPE_DIM = 32
HALF_ROPE = ROPE_DIM // 2
V_DIM = 64
Q_RANK = 384
KV_RANK = 256
ROPE_THETA = 10000.0
SG_GROUPS = 4
SG_WIDTH = 256
CHUNK = 128
CV_WIDTH = 256
MLA_WIDTH = N_HEADS * V_DIM
D_FF = 2816
EPS = 1e-6

OFF_CQ = 0
OFF_CKV = OFF_CQ + Q_RANK
OFF_KR = OFF_CKV + KV_RANK
OFF_SG = OFF_KR + ROPE_DIM
OFF_CV = OFF_SG + 2 * SG_WIDTH
IN_WIDTH = OFF_CV + 3 * CV_WIDTH

P_CQ = 0
P_SG = 384
P_CKV = 896
P_KR = 1152
P_CV = 1280
P_WIDTH = 2048

HEAD_PAD = 128
QK_WIDTH = N_HEADS * HEAD_PAD
BF16_ROWS = 16
V_SLOT = V_DIM + BF16_ROWS
VT_ROWS = N_HEADS * V_SLOT

VMEM_LIMIT_BYTES = 56 * 1024 * 1024
LANES = 128

TM_IN = 1024
T_ATT = 256
TM_OUT = 512
FF_CHUNKS = ((0, 1024), (1024, 1024), (2048, 768))

_NT = (((1,), (1,)), ((), ()))
SOFTMAX_SCALE = (NOPE_DIM + ROPE_DIM) ** -0.5
LOG2E = 1.4426950408889634
MASK_VALUE = -1e30
PV_LAG = 2


def _dot(a, b):
    return jnp.dot(a, b, preferred_element_type=jnp.float32)


def _dot_nt(a, b):
    return lax.dot_general(a, b, _NT, preferred_element_type=jnp.float32)


def _rms(x, g):
    return x * lax.rsqrt(jnp.mean(x * x, axis=-1, keepdims=True) + EPS) * g


def _const_spec(shape):
    nd = len(shape)
    return pl.BlockSpec(shape, lambda *_: (0,) * nd, pipeline_mode=pl.Buffered(1))


def _rope_kernel(pos_ref, inv_ref, cos_ref, sin_ref):
    ang = pos_ref[...].astype(jnp.float32) * inv_ref[...]
    cos_ref[...] = jnp.cos(ang)
    sin_ref[...] = jnp.sin(ang)


def _rope_tables(positions):
    t = positions.size
    tn = 4096
    pos = positions.reshape(1, t)
    inv_freq = 1.0 / (ROPE_THETA ** (jnp.arange(0, HALF_ROPE, dtype=jnp.float32) / HALF_ROPE))
    return pl.pallas_call(
        _rope_kernel,
        grid=(t // tn,),
        in_specs=[pl.BlockSpec((1, tn), lambda i: (0, i)),
                  pl.BlockSpec((HALF_ROPE, 1), lambda i: (0, 0))],
        out_specs=[pl.BlockSpec((HALF_ROPE, tn), lambda i: (0, i))] * 2,
        out_shape=[jax.ShapeDtypeStruct((HALF_ROPE, t), jnp.float32)] * 2,
        name="rope_tables",
    )(pos, inv_freq.reshape(HALF_ROPE, 1))


def _inproj_kernel(steps_per_seq,
                   x_ref, g_ref, win_ref, qg_ref, wuqt_ref, kvg_ref, wk_ref, wvt_ref,
                   ropec_ref, ropes_ref, cost_ref, sint_ref, lng_ref, lnb_ref, avg_ref, wsp_ref,
                   bsp_ref, convw_ref, ong_ref, wout_f32_ref, wgate_f32_ref, wup_f32_ref,
                   wdown_f32_ref,
                   qt_ref, k_ref, vt_ref, ybc_ref, wout_ref, wgate_ref, wup_ref, wdown_ref,
                   carry_ref):
    tm = x_ref.shape[0]
    th = tm // 2
    halves = (slice(0, th), slice(th, tm))

    @pl.when(pl.program_id(0) % steps_per_seq == 0)
    def _():
        carry_ref[...] = jnp.zeros_like(carry_ref)

    for src_ref, dst_ref in ((wout_f32_ref, wout_ref), (wgate_f32_ref, wgate_ref),
                             (wup_f32_ref, wup_ref), (wdown_f32_ref, wdown_ref)):
        dst_ref[...] = src_ref[...].astype(jnp.bfloat16)

    avg = avg_ref[...]
    ong = ong_ref[...]
    cw = convw_ref[...]
    bsp = bsp_ref[...]
    tk = vt_ref.shape[2]

    def group_mean(a):
        hi = a.astype(jnp.bfloat16)
        lo = (a - hi.astype(jnp.float32)).astype(jnp.bfloat16)
        return _dot(hi, avg) + _dot(lo, avg)

    z = [_dot(_rms(x_ref[rows, :], g_ref[...]).astype(jnp.bfloat16), win_ref[...])
         for rows in halves]

    for zh, rows in zip(z, halves):
        c_q = _rms(zh[:, P_CQ:P_CQ + Q_RANK], qg_ref[...]).astype(jnp.bfloat16)
        qt = _dot_nt(wuqt_ref[...], c_q)
        cos_t = cost_ref[:, rows]
        sin_t = sint_ref[:, rows]
        for hd in range(N_HEADS):
            base = hd * HEAD_PAD
            t1 = qt[base + NOPE_DIM:base + NOPE_DIM + HALF_ROPE]
            t2 = qt[base + NOPE_DIM + HALF_ROPE:base + NOPE_DIM + ROPE_DIM]
            qt_ref[base:base + NOPE_DIM, rows] = qt[base:base + NOPE_DIM].astype(jnp.bfloat16)
            qt_ref[base + NOPE_DIM:base + NOPE_DIM + HALF_ROPE, rows] = (
                t1 * cos_t - t2 * sin_t).astype(jnp.bfloat16)
            qt_ref[base + NOPE_DIM + HALF_ROPE:base + NOPE_DIM + ROPE_DIM, rows] = (
                t2 * cos_t + t1 * sin_t).astype(jnp.bfloat16)
            qt_ref[base + NOPE_DIM + ROPE_DIM:base + HEAD_PAD, rows] = jnp.zeros(
                (HEAD_PAD - NOPE_DIM - ROPE_DIM, th), jnp.bfloat16)

    u, dv = [], []
    for zh in z:
        uv = zh[:, P_SG:P_SG + 2 * SG_WIDTH]
        uv = uv * (0.5 * (1.0 + jnp.tanh(
            math.sqrt(2.0 / math.pi) * (uv + 0.044715 * (uv * uv * uv)))))
        u.append(uv[:, :SG_WIDTH])
        v = uv[:, SG_WIDTH:]
        dv.append(v - group_mean(v))

    c_kv = []
    for zh, rows in zip(z, halves):
        c_kv.append(_rms(zh[:, P_CKV:P_CKV + KV_RANK], kvg_ref[...]).astype(jnp.bfloat16))
        zkr = zh[:, P_KR:P_KR + LANES]
        k_rope = (zkr * ropec_ref[rows, :]
                  + pltpu.roll(zkr, HALF_ROPE, 1) * ropes_ref[rows, :])
        k_nope = _dot(c_kv[-1], wk_ref[...])
        for hd in range(N_HEADS):
            slot = slice(hd * HEAD_PAD, (hd + 1) * HEAD_PAD)
            k_ref[rows, slot] = (k_nope[:, slot] + k_rope).astype(jnp.bfloat16)

    vn = [(d * lax.rsqrt(group_mean(d * d) + EPS) * lng_ref[...] + lnb_ref[...]
           ).astype(jnp.bfloat16) for d in dv]

    prev = carry_ref[...]
    row8 = lax.broadcasted_iota(jnp.int32, (8, CV_WIDTH), 0)
    for zh, rows in zip(z, halves):
        gate_b = zh[:, P_CV:P_CV + CV_WIDTH]
        y = (zh[:, P_CV + CV_WIDTH:P_CV + 2 * CV_WIDTH]
             * zh[:, P_CV + 2 * CV_WIDTH:P_CV + 3 * CV_WIDTH])

        def shifted(k, y=y, prev=prev):
            r = pltpu.roll(y, k, 0)
            top = jnp.where(row8 < k, pltpu.roll(prev, k, 0), r[:8])
            return jnp.concatenate([top, r[8:]], axis=0)

        conv = shifted(2) * cw[0:1] + shifted(1) * cw[1:2] + y * cw[2:3]
        ybc_ref[rows, SG_WIDTH:] = _rms(gate_b * conv, ong[:, SG_WIDTH:]).astype(jnp.bfloat16)
        prev = y[th - 8:]
    carry_ref[...] = prev

    rows_i = lax.broadcasted_iota(jnp.int32, (SG_GROUPS * CHUNK, CHUNK), 0)
    cols_i = lax.broadcasted_iota(jnp.int32, (SG_GROUPS * CHUNK, CHUNK), 1)
    w_causal = jnp.where((rows_i % CHUNK) >= cols_i, wsp_ref[...], 0.0).astype(jnp.bfloat16)
    lane_group = lax.broadcasted_iota(jnp.int32, (CHUNK, SG_WIDTH), 1) // HEAD_DIM
    mixed = []
    for vh in vn:
        chunks = []
        for c in range(th // CHUNK):
            o = _dot(w_causal, vh[c * CHUNK:(c + 1) * CHUNK])
            m = o[3 * CHUNK:]
            for g in (2, 1, 0):
                m = jnp.where(lane_group == g, o[g * CHUNK:(g + 1) * CHUNK], m)
            chunks.append(m + bsp)
        mixed.append(jnp.concatenate(chunks, axis=0))

    ones = jnp.ones((BF16_ROWS, tk), jnp.bfloat16)
    for hf, ckv in enumerate(c_kv):
        vt = _dot_nt(wvt_ref[...], ckv).astype(jnp.bfloat16)
        for j in range(th // tk):
            blk = hf * (th // tk) + j
            for hd in range(N_HEADS):
                vt_ref[blk, hd * V_SLOT:hd * V_SLOT + V_DIM, :] = vt[hd * V_DIM:(hd + 1) * V_DIM,
                                                                     j * tk:(j + 1) * tk]
                vt_ref[blk, hd * V_SLOT + V_DIM:(hd + 1) * V_SLOT, :] = ones

    for uh, mh, rows in zip(u, mixed, halves):
        ybc_ref[rows, :SG_WIDTH] = _rms(uh * mh, ong[:, :SG_WIDTH]).astype(jnp.bfloat16)


def _inproj(x, lw, tabs, seq):
    t = x.shape[0]
    tm = TM_IN
    rope_c, rope_s, cos_t, sin_t = tabs
    row = lambda i: (i, 0)
    col = lambda i: (0, i)
    consts = [lw["g_pre"], lw["w_in"], lw["q_norm_g"], lw["w_uq_t"], lw["kv_norm_g"], lw["w_k"],
              lw["w_v_t"]]
    consts2 = [lw["sg_ln_g"], lw["sg_ln_b"], lw["avg"], lw["w_sp"], lw["b_sp"], lw["conv_w"],
               lw["out_norm_g_bc"]]
    in_specs = ([pl.BlockSpec((tm, D_MODEL), row)]
                + [_const_spec(c.shape) for c in consts]
                + [pl.BlockSpec((tm, LANES), row),
                   pl.BlockSpec((tm, LANES), row),
                   pl.BlockSpec((HALF_ROPE, tm), col),
                   pl.BlockSpec((HALF_ROPE, tm), col)]
                + [_const_spec(c.shape) for c in consts2])
    steps = t // tm
    layer, casts = lw["layer"], lw["stacked_f32"]
    cast_in_specs, cast_out_specs = [], []
    for w in casts:
        n_rows, n_cols = w.shape[1:]
        rows = -(-n_rows // (steps * BF16_ROWS)) * BF16_ROWS
        while n_rows % rows:
            rows += BF16_ROWS
        last = n_rows // rows - 1
        cast_in_specs.append(pl.BlockSpec(
            (None, rows, n_cols), lambda i, last=last: (layer, jnp.minimum(i, last), 0)))
        cast_out_specs.append(pl.BlockSpec(
            (rows, n_cols), lambda i, last=last: (jnp.minimum(i, last), 0)))
    in_specs = in_specs + cast_in_specs
    out_shape = [jax.ShapeDtypeStruct((QK_WIDTH, t), jnp.bfloat16),
                 jax.ShapeDtypeStruct((t, QK_WIDTH), jnp.bfloat16),
                 jax.ShapeDtypeStruct((t // T_ATT, VT_ROWS, T_ATT), jnp.bfloat16),
                 jax.ShapeDtypeStruct((t, SG_WIDTH + CV_WIDTH), jnp.bfloat16)]
    out_shape += [jax.ShapeDtypeStruct(w.shape[1:], jnp.bfloat16) for w in casts]
    out_specs = [pl.BlockSpec((QK_WIDTH, tm), col),
                 pl.BlockSpec((tm, QK_WIDTH), row),
                 pl.BlockSpec((tm // T_ATT, VT_ROWS, T_ATT), lambda i: (i, 0, 0)),
                 pl.BlockSpec((tm, SG_WIDTH + CV_WIDTH), row)]
    out_specs += cast_out_specs
    return pl.pallas_call(
        functools.partial(_inproj_kernel, seq // tm),
        grid=(t // tm,),
        in_specs=in_specs,
        out_specs=out_specs,
        out_shape=out_shape,
        scratch_shapes=[pltpu.VMEM((8, CV_WIDTH), jnp.float32)],
        compiler_params=pltpu.CompilerParams(
            dimension_semantics=("arbitrary",), vmem_limit_bytes=VMEM_LIMIT_BYTES),
        name="inproj",
    )(x, *consts, rope_c, rope_s, cos_t, sin_t, *consts2, *casts)


def _attn_kernel(qt_ref, k_ref, vt_ref, o_ref, m_ref, acc_ref, sa_ref, sb_ref):
    tk = sa_ref.shape[1]
    g = pl.program_id(1)
    m_ref[...] = jnp.full(m_ref.shape, MASK_VALUE, jnp.float32)
    acc_ref[...] = jnp.zeros(acc_ref.shape, jnp.float32)
    lo = slice(0, tk)
    hi = slice(tk, 2 * tk)

    def stage(kb_next, dst_ref, cols_next, kb_cur, src_ref, cols_cur, masked=False):
        probs, m_all = [], []
        if kb_cur is not None:
            if masked:
                key_row = lax.broadcasted_iota(jnp.int32, (tk, tk), 0)
                qry_col = lax.broadcasted_iota(jnp.int32, (tk, tk), 1)
                causal = key_row <= qry_col
            for hd in range(N_HEADS):
                s = src_ref[hd]
                if masked:
                    s = jnp.where(causal, s, MASK_VALUE)
                m = m_ref[hd:hd + 1, cols_cur]
                m_new = jnp.maximum(m, jnp.max(s, axis=0, keepdims=True))
                probs.append((jnp.exp2(m - m_new), jnp.exp2(s - m_new).astype(jnp.bfloat16)))
                m_all.append(m_new)

        def score_dot(hd):
            start = pl.multiple_of(kb_next * tk, tk)
            q_h = qt_ref[hd * HEAD_PAD:(hd + 1) * HEAD_PAD, cols_next]
            k_blk = k_ref[pl.ds(start, tk), hd * HEAD_PAD:(hd + 1) * HEAD_PAD]
            dst_ref[hd] = _dot(k_blk, q_h)

        def pv_dot(hd):
            alpha, p = probs[hd]
            slab = slice(hd * V_SLOT, (hd + 1) * V_SLOT)
            v_blk = vt_ref[kb_cur, slab, :]
            acc_ref[slab, cols_cur] = alpha * acc_ref[slab, cols_cur] + _dot(v_blk, p)

        for j in range(N_HEADS + PV_LAG):
            if kb_next is not None and j < N_HEADS:
                score_dot(j)
            if kb_cur is not None and j >= PV_LAG:
                pv_dot(j - PV_LAG)
        for hd in range(len(m_all)):
            m_ref[hd:hd + 1, cols_cur] = m_all[hd]

    def unmasked_blocks(cols, first_ref, second_ref):
        def pair(t, carry):
            kb = 2 * t
            stage(kb + 1, second_ref, cols, kb, first_ref, cols)
            stage(kb + 2, first_ref, cols, kb + 1, second_ref, cols)
            return carry

        def quad(t, carry):
            pair(2 * t, carry)
            return pair(2 * t + 1, carry)

        def octet(t, carry):
            quad(2 * t, carry)
            return quad(2 * t + 1, carry)

        lax.fori_loop(0, g // 4, octet, 0)
        lax.fori_loop(2 * (g // 4), g // 2, quad, 0)
        lax.fori_loop(2 * (g // 2), g, pair, 0)

    stage(0, sa_ref, lo, None, None, None)
    unmasked_blocks(lo, sa_ref, sb_ref)
    stage(0, sb_ref, hi, 2 * g, sa_ref, lo, masked=True)
    unmasked_blocks(hi, sb_ref, sa_ref)
    stage(2 * g + 1, sa_ref, hi, 2 * g, sb_ref, hi)
    stage(None, None, None, 2 * g + 1, sa_ref, hi, masked=True)

    outs = []
    for hd in range(N_HEADS):
        base = hd * V_SLOT
        outs.append(acc_ref[base:base + V_DIM, :] / acc_ref[base + V_DIM:base + V_DIM + 1, :])
    o_ref[...] = jnp.concatenate(outs, axis=0).T.astype(o_ref.dtype)


def _attention(qt, k, vt, batch, seq):
    t = k.shape[0]
    tk = T_ATT
    tq = 2 * tk
    nq = seq // tq
    nk = seq // tk
    return pl.pallas_call(
        _attn_kernel,
        grid=(batch, nq),
        in_specs=[pl.BlockSpec((QK_WIDTH, tq), lambda b, i: (0, b * nq + i)),
                  pl.BlockSpec((seq, QK_WIDTH), lambda b, i: (b, 0)),
                  pl.BlockSpec((nk, VT_ROWS, tk), lambda b, i: (b, 0, 0))],
        out_specs=pl.BlockSpec((tq, MLA_WIDTH), lambda b, i: (b * nq + i, 0)),
        out_shape=jax.ShapeDtypeStruct((t, MLA_WIDTH), jnp.bfloat16),
        scratch_shapes=[pltpu.VMEM((N_HEADS, tq), jnp.float32),
                        pltpu.VMEM((VT_ROWS, tq), jnp.float32),
                        pltpu.VMEM((N_HEADS, tk, tk), jnp.float32),
                        pltpu.VMEM((N_HEADS, tk, tk), jnp.float32)],
        compiler_params=pltpu.CompilerParams(
            dimension_semantics=("arbitrary", "arbitrary"), vmem_limit_bytes=VMEM_LIMIT_BYTES),
        name="attention",
    )(qt, k, vt)


def _outffn_kernel(x_ref, ya_ref, ybc_ref, onga_ref, wout_ref, gpost_ref, gffn_ref,
                   wgate_ref, wup_ref, wdown_ref, gffn_post_ref, o_ref):
    tm = x_ref.shape[0]
    halves = (slice(0, tm // 2), slice(tm // 2, tm))
    x1, h = [], []
    for rows in halves:
        ya = _rms(ya_ref[rows, :].astype(jnp.float32), onga_ref[...]).astype(jnp.bfloat16)
        o = _dot(ya, wout_ref[:MLA_WIDTH, :]) + _dot(ybc_ref[rows, :], wout_ref[MLA_WIDTH:, :])
        x1.append(x_ref[rows, :] + _rms(o, gpost_ref[...]))
    for k in range(2):
        h.append(_rms(x1[k], gffn_ref[...]).astype(jnp.bfloat16))
    d = [jnp.zeros(x1[0].shape, jnp.float32)] * 2
    for start, size in FF_CHUNKS:
        gate_up = [(_dot(h[k], wgate_ref[:, start:start + size]),
                    _dot(h[k], wup_ref[:, start:start + size])) for k in range(2)]
        for k in range(2):
            gate, up = gate_up[k]
            f = (gate * (1.0 / (1.0 + jnp.exp(-gate))) * up).astype(jnp.bfloat16)
            d[k] = d[k] + _dot(f, wdown_ref[start:start + size, :])
    for k, rows in enumerate(halves):
        o_ref[rows, :] = x1[k] + _rms(d[k], gffn_post_ref[...])


def _outffn(x, ya, ybc, lw, w_bf16):
    t = x.shape[0]
    tm = TM_OUT
    row = lambda i: (i, 0)
    w_out, w_gate, w_up, w_down = w_bf16
    consts = [lw["out_norm_g_a"], w_out, lw["g_post"], lw["g_ffn_pre"], w_gate,
              w_up, w_down, lw["g_ffn_post"]]
    return pl.pallas_call(
        _outffn_kernel,
        grid=(t // tm,),
        in_specs=[pl.BlockSpec((tm, D_MODEL), row),
                  pl.BlockSpec((tm, MLA_WIDTH), row),
                  pl.BlockSpec((tm, SG_WIDTH + CV_WIDTH), row)]
                 + [_const_spec(c.shape) for c in consts],
        out_specs=pl.BlockSpec((tm, D_MODEL), row),
        out_shape=jax.ShapeDtypeStruct((t, D_MODEL), jnp.float32),
        compiler_params=pltpu.CompilerParams(
            dimension_semantics=("arbitrary",), vmem_limit_bytes=VMEM_LIMIT_BYTES),
        name="outffn",
    )(x, ya, ybc, *consts)


def _prep_layer(l, p):
    bf = jnp.bfloat16
    f32 = jnp.float32
    w_in = p["w_in"][l]
    d = w_in.shape[0]
    t1 = w_in[:, OFF_KR:OFF_KR + HALF_ROPE]
    t2 = w_in[:, OFF_KR + HALF_ROPE:OFF_SG]
    w_in_p = jnp.concatenate([
        w_in[:, OFF_CQ:OFF_CKV],
        w_in[:, OFF_SG:OFF_CV],
        w_in[:, OFF_CKV:OFF_KR],
        jnp.zeros((d, NOPE_DIM - HALF_ROPE), f32), t2, t1, t2,
        jnp.zeros((d, HEAD_PAD - NOPE_DIM - ROPE_DIM), f32),
        w_in[:, OFF_CV:IN_WIDTH]], axis=1).astype(bf)

    w_uq = p["w_uq"][l].reshape(Q_RANK, N_HEADS, NOPE_DIM + ROPE_DIM) * (SOFTMAX_SCALE * LOG2E)
    w_uq = jnp.pad(w_uq, ((0, 0), (0, 0), (0, HEAD_PAD - NOPE_DIM - ROPE_DIM)))
    w_uq_t = w_uq.reshape(Q_RANK, QK_WIDTH).T.astype(bf)

    w_ukv = p["w_ukv"][l].reshape(KV_RANK, N_HEADS, NOPE_DIM + V_DIM)
    w_k = jnp.pad(w_ukv[:, :, :NOPE_DIM], ((0, 0), (0, 0), (0, HEAD_PAD - NOPE_DIM)))
    w_k = w_k.reshape(KV_RANK, QK_WIDTH).astype(bf)
    w_v_t = w_ukv[:, :, NOPE_DIM:].reshape(KV_RANK, MLA_WIDTH).T.astype(bf)

    g = p["out_norm_g"][l]
    return {
        "g_pre": p["mix_pre_g"][l].reshape(1, -1),
        "w_in": w_in_p,
        "q_norm_g": p["q_norm_g"][l].reshape(1, -1),
        "w_uq_t": w_uq_t,
        "kv_norm_g": p["kv_norm_g"][l].reshape(1, -1),
        "w_k": w_k,
        "w_v_t": w_v_t,
        "sg_ln_g": p["sg_ln_g"][l].reshape(1, -1),
        "sg_ln_b": p["sg_ln_b"][l].reshape(1, -1),
        "w_sp": p["w_sp"][l].reshape(SG_GROUPS * CHUNK, CHUNK),
        "b_sp": jnp.repeat(p["b_sp"][l].T, HEAD_DIM, axis=1),
        "conv_w": p["conv_w"][l],
        "out_norm_g_a": g[:MLA_WIDTH].reshape(1, -1),
        "out_norm_g_bc": g[MLA_WIDTH:].reshape(1, -1),
        "layer": l,
        "stacked_f32": [p["w_out"], p["w_gate"], p["w_up"], p["w_down"]],
        "g_post": p["mix_post_g"][l].reshape(1, -1),
        "g_ffn_pre": p["ffn_pre_g"][l].reshape(1, -1),
        "g_ffn_post": p["ffn_post_g"][l].reshape(1, -1),
    }


def _group_average_matrix():
    grp = jnp.arange(SG_WIDTH) // HEAD_DIM
    return ((grp[:, None] == grp[None, :]).astype(jnp.float32) / HEAD_DIM).astype(jnp.bfloat16)


def _key_rope_tables(cos_r, sin_r):
    t = cos_r.shape[0]
    left = jnp.zeros((t, NOPE_DIM), jnp.float32)
    right = jnp.zeros((t, HEAD_PAD - NOPE_DIM - ROPE_DIM), jnp.float32)
    rope_c = jnp.concatenate([left, cos_r, cos_r, right], axis=1)
    rope_s = jnp.concatenate([left, -sin_r, sin_r, right], axis=1)
    return rope_c, rope_s


def kernel(x, positions, mix_pre_g, mix_post_g, ffn_pre_g, ffn_post_g, w_in, q_norm_g, w_uq,
           kv_norm_g, w_ukv, sg_ln_g, sg_ln_b, w_sp, b_sp, conv_w, out_norm_g, w_out,
           w_gate, w_up, w_down):
    params = dict(mix_pre_g=mix_pre_g, mix_post_g=mix_post_g, ffn_pre_g=ffn_pre_g,
                  ffn_post_g=ffn_post_g, w_in=w_in, q_norm_g=q_norm_g, w_uq=w_uq,
                  kv_norm_g=kv_norm_g, w_ukv=w_ukv, sg_ln_g=sg_ln_g, sg_ln_b=sg_ln_b,
                  w_sp=w_sp, b_sp=b_sp, conv_w=conv_w, out_norm_g=out_norm_g, w_out=w_out,
                  w_gate=w_gate, w_up=w_up, w_down=w_down)
    batch, seq, d = x.shape
    depth = w_in.shape[0]
    assert seq % TM_IN == 0 and seq % (2 * T_ATT) == 0 and TM_IN % T_ATT == 0
    assert (batch * seq) % TM_OUT == 0 and d == D_MODEL

    cos_t, sin_t = _rope_tables(positions)
    rope_c, rope_s = _key_rope_tables(cos_t.T, sin_t.T)
    avg = _group_average_matrix()

    xf = x.reshape(batch * seq, d)
    for l in range(depth):
        lw = _prep_layer(l, params)
        lw["avg"] = avg
        qt, k, vt, ybc, *w_bf16 = _inproj(xf, lw, (rope_c, rope_s, cos_t, sin_t), seq)
        ya = _attention(qt, k, vt, batch, seq)
        xf = _outffn(xf, ya, ybc, lw, w_bf16)
    return xf.reshape(batch, seq, d)
```
